```python
import math
import jax, jax.numpy as jnp
from jax import lax
import numpy as np

D_MODEL = 1024
BATCH = 2
SEQ = 8192
DEPTH = 1

HEAD_DIM = 64
N_HEADS_MOBA = 8
N_HEADS_SB = 8
W_MOBA = N_HEADS_MOBA * HEAD_DIM
W_SB = N_HEADS_SB * HEAD_DIM
MOBA_BLOCK = 256
MOBA_TOPK = 3
Q_CHUNK = 128
PEER_HEADS = 8
PEER_NKEYS = 128
PEER_N_EXPERTS = PEER_NKEYS * PEER_NKEYS
PEER_QDIM = 256
PEER_HALF = PEER_QDIM // 2
PEER_TOPK = 16
PEER_TOK_CHUNK = 128
RMS_EPS = 1e-6
IN_SPLITS = [W_MOBA, W_MOBA, W_MOBA, W_SB, W_SB, W_SB, D_MODEL, D_MODEL]
IN_WIDTH = sum(IN_SPLITS)

kernel_name = "hybrid_moba_stickbreak_peer_block"


def rmsnorm(x, g):
    xf = x.astype(jnp.float32)
    y = xf * lax.rsqrt(jnp.mean(xf * xf, axis=-1, keepdims=True) + RMS_EPS)
    return (y * g.astype(jnp.float32)).astype(x.dtype)


def alibi_slopes(n_heads):
    return jnp.exp2(-8.0 * jnp.arange(1, n_heads + 1, dtype=jnp.float32) / n_heads)


def _heads(t, n):
    b, s, _ = t.shape
    return t.reshape(b, s, n, HEAD_DIM).transpose(0, 2, 1, 3)


def _merge(t):
    b, h, s, dh = t.shape
    return t.transpose(0, 2, 1, 3).reshape(b, s, h * dh)


def moba_attention(q, k, v):
    B, H, S, dh = q.shape
    nb = -(-S // MOBA_BLOCK)
    pad = nb * MOBA_BLOCK - S
    kp = jnp.pad(k, ((0, 0), (0, 0), (0, pad), (0, 0)))
    vp = jnp.pad(v, ((0, 0), (0, 0), (0, pad), (0, 0)))
    kb = kp.reshape(B, H, nb, MOBA_BLOCK, dh)
    vb = vp.reshape(B, H, nb, MOBA_BLOCK, dh)
    kmean = jnp.mean(kb.astype(jnp.float32), axis=3)
    ksel = min(MOBA_TOPK, nb)
    scale = dh ** -0.5
    slope = alibi_slopes(H).reshape(1, H, 1, 1)
    slope5 = slope[..., None]
    bidx = jnp.arange(B)[:, None, None, None]
    hidx = jnp.arange(H)[None, :, None, None]
    n_chunks = S // Q_CHUNK
    qc = q.reshape(B, H, n_chunks, Q_CHUNK, dh).transpose(2, 0, 1, 3, 4)

    def chunk(args):
        q_blk, c = args
        qf = q_blk.astype(jnp.float32)
        t = c * Q_CHUNK + jnp.arange(Q_CHUNK)
        own = (c * Q_CHUNK) // MOBA_BLOCK
        bs = jnp.einsum('bhqd,bhnd->bhqn', qf, kmean)
        bs = jnp.where(jnp.arange(nb) < own, bs, -jnp.inf)
        _, top_i = lax.top_k(bs, ksel)
        valid = top_i < own
        k_sel = kb[bidx, hidx, top_i].astype(jnp.float32)
        v_sel = vb[bidx, hidx, top_i].astype(jnp.float32)
        s_pos = top_i[..., None] * MOBA_BLOCK + jnp.arange(MOBA_BLOCK)
        dist_sel = (t[:, None, None] - s_pos).astype(jnp.float32)
        sc_sel = jnp.einsum('bhqd,bhqnkd->bhqnk', qf, k_sel) * scale - slope5 * dist_sel
        sc_sel = jnp.where(valid[..., None], sc_sel, -jnp.inf).reshape(B, H, Q_CHUNK, ksel * MOBA_BLOCK)
        k_own = lax.dynamic_slice_in_dim(kp, own * MOBA_BLOCK, MOBA_BLOCK, axis=2).astype(jnp.float32)
        v_own = lax.dynamic_slice_in_dim(vp, own * MOBA_BLOCK, MOBA_BLOCK, axis=2).astype(jnp.float32)
        s_own = own * MOBA_BLOCK + jnp.arange(MOBA_BLOCK)
        dist_own = (t[:, None] - s_own[None, :]).astype(jnp.float32)
        sc_own = jnp.einsum('bhqd,bhkd->bhqk', qf, k_own) * scale - slope * dist_own
        sc_own = jnp.where(s_own[None, :] <= t[:, None], sc_own, -jnp.inf)
        p = jax.nn.softmax(jnp.concatenate([sc_sel, sc_own], axis=-1), axis=-1)
        p_sel = p[..., :ksel * MOBA_BLOCK].reshape(B, H, Q_CHUNK, ksel, MOBA_BLOCK)
        p_own = p[..., ksel * MOBA_BLOCK:]
        o = (jnp.einsum('bhqnk,bhqnkd->bhqd', p_sel, v_sel)
             + jnp.einsum('bhqk,bhkd->bhqd', p_own, v_own))
        return o.astype(q.dtype)

    out = lax.map(chunk, (qc, jnp.arange(n_chunks)))
    return out.transpose(1, 2, 0, 3, 4).reshape(B, H, S, dh)


def stick_breaking_attention(q, k, v):
    B, H, S, dh = q.shape
    scale = dh ** -0.5
    kf = k.astype(jnp.float32)
    vf = v.astype(jnp.float32)
    s_idx = jnp.arange(S)
    n_chunks = S // Q_CHUNK
    qc = q.reshape(B, H, n_chunks, Q_CHUNK, dh).transpose(2, 0, 1, 3, 4)

    def chunk(args):
        q_blk, c = args
        t = c * Q_CHUNK + jnp.arange(Q_CHUNK)
        z = jnp.einsum('bhqd,bhkd->bhqk', q_blk.astype(jnp.float32), kf) * scale
        strict = s_idx[None, :] < t[:, None]
        log_1m = jnp.where(strict, jax.nn.log_sigmoid(-z), 0.0)
        suffix = lax.cumsum(log_1m, axis=3, reverse=True) - log_1m
        w = jnp.where(strict, jnp.exp(jax.nn.log_sigmoid(z) + suffix), 0.0)
        return jnp.einsum('bhqk,bhkd->bhqd', w, vf).astype(q.dtype)

    out = lax.map(chunk, (qc, jnp.arange(n_chunks)))
    return out.transpose(1, 2, 0, 3, 4).reshape(B, H, S, dh)


def peer_ffn(x, w_q, sub_keys, expert_u, expert_v):
    B, S, D = x.shape
    T = B * S
    xt = x.reshape(T // PEER_TOK_CHUNK, PEER_TOK_CHUNK, D)
    skf = sub_keys.astype(jnp.float32)

    def chunk(x_blk):
        tc = x_blk.shape[0]
        qh = (x_blk @ w_q).astype(jnp.float32).reshape(tc, PEER_HEADS, 2, PEER_HALF)
        scores = jnp.einsum('thpc,hpnc->thpn', qh, skf)
        s_top, i_top = lax.top_k(scores, PEER_TOPK)
        cand = s_top[:, :, 0, :, None] + s_top[:, :, 1, None, :]
        cand_idx = i_top[:, :, 0, :, None] * PEER_NKEYS + i_top[:, :, 1, None, :]
        c_s, c_pos = lax.top_k(cand.reshape(tc, PEER_HEADS, PEER_TOPK * PEER_TOPK), PEER_TOPK)
        e_idx = jnp.take_along_axis(cand_idx.reshape(tc, PEER_HEADS, PEER_TOPK * PEER_TOPK), c_pos, axis=-1)
        g = jax.nn.softmax(c_s, axis=-1)
        u = expert_u[e_idx].astype(jnp.float32)
        v = expert_v[e_idx].astype(jnp.float32)
        a = jax.nn.gelu(jnp.einsum('td,thkd->thk', x_blk.astype(jnp.float32), u), approximate=False)
        return jnp.einsum('thk,thkd->td', g * a, v).astype(x.dtype)

    return lax.map(chunk, xt).reshape(B, S, D)


def setup_inputs(seed: int = 0) -> dict:
    key = jax.random.key(seed)
    ks = jax.random.split(key, 12)
    f32 = jnp.float32
    x = jax.random.normal(ks[0], (BATCH, SEQ, D_MODEL), f32)
    norm1_g = 1.0 + 0.01 * jax.random.normal(ks[1], (DEPTH, D_MODEL), f32)
    w_in = jax.random.normal(ks[2], (DEPTH, D_MODEL, IN_WIDTH), f32) * D_MODEL ** -0.5
    w_out_moba = jax.random.normal(ks[3], (DEPTH, W_MOBA, D_MODEL), f32) * W_MOBA ** -0.5
    w_out_sb = jax.random.normal(ks[4], (DEPTH, W_SB, D_MODEL), f32) * W_SB ** -0.5
    w_mix_out = jax.random.normal(ks[5], (DEPTH, D_MODEL, D_MODEL), f32) * D_MODEL ** -0.5
    norm2_g = 1.0 + 0.01 * jax.random.normal(ks[6], (DEPTH, D_MODEL), f32)
    peer_w_q = jax.random.normal(ks[7], (DEPTH, D_MODEL, PEER_HEADS * PEER_QDIM), f32) * D_MODEL ** -0.5
    peer_sub_keys = jax.random.normal(ks[8], (DEPTH, PEER_HEADS, 2, PEER_NKEYS, PEER_HALF), f32) * PEER_HALF ** -0.5
    peer_u = jax.random.normal(ks[9], (DEPTH, PEER_N_EXPERTS, D_MODEL), f32) * D_MODEL ** -0.5
    peer_v = jax.random.normal(ks[10], (DEPTH, PEER_N_EXPERTS, D_MODEL), f32) * PEER_HEADS ** -0.5
    final_norm_g = 1.0 + 0.01 * jax.random.normal(ks[11], (D_MODEL,), f32)
    return {"x": x, "norm1_g": norm1_g, "w_in": w_in, "w_out_moba": w_out_moba,
            "w_out_sb": w_out_sb, "w_mix_out": w_mix_out, "norm2_g": norm2_g,
            "peer_w_q": peer_w_q, "peer_sub_keys": peer_sub_keys, "peer_u": peer_u,
            "peer_v": peer_v, "final_norm_g": final_norm_g}


def reference(x, norm1_g, w_in, w_out_moba, w_out_sb, w_mix_out, norm2_g,
              peer_w_q, peer_sub_keys, peer_u, peer_v, final_norm_g):
    split_at = [int(i) for i in np.cumsum(IN_SPLITS)[:-1]]
    h = x
    for l in range(DEPTH):
        xn = rmsnorm(h, norm1_g[l])
        proj = xn @ w_in[l]
        q_a, k_a, v_a, q_b, k_b, v_b, gate_a, gate_b = jnp.split(proj, split_at, axis=-1)
        y_a = _merge(moba_attention(_heads(q_a, N_HEADS_MOBA), _heads(k_a, N_HEADS_MOBA),
                                    _heads(v_a, N_HEADS_MOBA))) @ w_out_moba[l]
        y_b = _merge(stick_breaking_attention(_heads(q_b, N_HEADS_SB), _heads(k_b, N_HEADS_SB),
                                              _heads(v_b, N_HEADS_SB))) @ w_out_sb[l]
        mixed = jax.nn.sigmoid(gate_a) * y_a + jax.nn.sigmoid(gate_b) * y_b
        h = h + mixed @ w_mix_out[l]
        h = h + peer_ffn(rmsnorm(h, norm2_g[l]), peer_w_q[l], peer_sub_keys[l], peer_u[l], peer_v[l])
    return rmsnorm(h, final_norm_g)
```

```python
import functools

import jax
import jax.numpy as jnp
from jax import lax
from jax.experimental import pallas as pl
from jax.experimental.pallas import tpu as pltpu

F32 = jnp.float32
BF16 = jnp.bfloat16

HEAD_DIM = 64
N_HEADS = 8
LANES = 128
MOBA_BLOCK = 256
MOBA_TOPK = 3
PEER_HEADS = 8
PEER_NKEYS = 128
PEER_TOPK = 16
RMS_EPS = 1e-6
NEG_BIG = -1e30
INV_SQRT2 = 0.7071067811865476
VMEM_LIMIT = 56 * 1024 * 1024

LANE_BLK0 = 64
LANE_QT_HI, LANE_QT_LO = 96, 97
LANE_KS_HI, LANE_KS_LO = 98, 99
MAX_KV_BLOCKS = LANE_QT_HI - LANE_BLK0

NT_DIMS = (((1,), (1,)), ((), ()))


def _cparams(sem):
    return pltpu.CompilerParams(dimension_semantics=sem, vmem_limit_bytes=VMEM_LIMIT)


def _rmsnorm(x, g):
    return x * lax.rsqrt(jnp.mean(x * x, axis=-1, keepdims=True) + RMS_EPS) * g


def _split_hi_lo(x):
    hi = x.astype(BF16)
    lo = (x - hi.astype(F32)).astype(BF16)
    return hi, lo


def _proj_kernel(x_ref, g_ref, w_ref, o_ref, xn_ref, *, per_head):
    j = pl.program_id(1)

    @pl.when(j == 0)
    def _():
        xn_ref[...] = _rmsnorm(x_ref[...], g_ref[...]).astype(BF16)

    res = jnp.dot(xn_ref[...], w_ref[...], preferred_element_type=F32)
    if per_head:
        res = res * jnp.where((j == 0) | (j == 3), HEAD_DIM ** -0.5, 1.0)
        for h in range(N_HEADS):
            o_ref[h] = res[:, h * LANES:(h + 1) * LANES].astype(BF16)
    else:
        o_ref[...] = jax.nn.sigmoid(res).astype(BF16)


def _proj(x2d, g, w, *, per_head, tm):
    T, D = x2d.shape
    ncol = w.shape[1] // 1024
    if per_head:
        out_shape = jax.ShapeDtypeStruct((ncol * N_HEADS, T, LANES), BF16)
        out_spec = pl.BlockSpec((N_HEADS, tm, LANES), lambda i, j: (j, i, 0))
    else:
        out_shape = jax.ShapeDtypeStruct((T, w.shape[1]), BF16)
        out_spec = pl.BlockSpec((tm, 1024), lambda i, j: (i, j))
    return pl.pallas_call(
        functools.partial(_proj_kernel, per_head=per_head),
        grid=(T // tm, ncol),
        in_specs=[pl.BlockSpec((tm, D), lambda i, j: (i, 0)),
                  pl.BlockSpec((1, D), lambda i, j: (0, 0)),
                  pl.BlockSpec((D, 1024), lambda i, j: (0, j))],
        out_specs=out_spec,
        out_shape=out_shape,
        scratch_shapes=[pltpu.VMEM((tm, D), BF16)],
        compiler_params=_cparams(("arbitrary", "arbitrary")),
        name="proj_heads" if per_head else "proj_gates",
    )(x2d, g, w)


def _moba_kernel(slope_ref, q_ref, k_ref, v_ref, o_ref, kaug_ref, kmean_ref, *, nb):
    bh = pl.program_id(0)
    i = pl.program_id(1)
    slope = slope_ref[bh]
    blk = MOBA_BLOCK

    @pl.when(i == 0)
    def _():
        kmean_ref[...] = jnp.zeros_like(kmean_ref)
        row = lax.broadcasted_iota(jnp.int32, (blk, LANES), 0)
        lane = lax.broadcasted_iota(jnp.int32, (blk, LANES), 1)
        ks = slope * row.astype(F32)
        ks_hi = ks.astype(BF16).astype(F32)
        ks_lo = ks - ks_hi
        base = jnp.where((lane == LANE_QT_HI) | (lane == LANE_QT_LO), 1.0, 0.0)
        base = jnp.where(lane == LANE_KS_HI, ks_hi, base)
        base = jnp.where(lane == LANE_KS_LO, ks_lo, base)

        def fill(n, c):
            sl = pl.ds(pl.multiple_of(n * blk, blk), blk)
            kf = k_ref[0, sl, :].astype(F32)
            kmean_ref[pl.ds(LANE_BLK0 + n, 1), :] = jnp.mean(kf, axis=0, keepdims=True)
            feat = jnp.where(lane == LANE_BLK0 + n, 1.0, base)
            kaug_ref[sl, :] = jnp.where(lane < HEAD_DIM, kf, feat).astype(BF16)
            return c

        lax.fori_loop(0, nb, fill, 0)

    q = q_ref[0]
    qf = q.astype(F32)
    tq = q.shape[0]
    bs = lax.dot_general(qf, kmean_ref[...], NT_DIMS, precision=lax.Precision.HIGHEST,
                         preferred_element_type=F32)
    lane = lax.broadcasted_iota(jnp.int32, (tq, LANES), 1)
    past = (lane >= LANE_BLK0) & (lane < LANE_BLK0 + i)
    bsm = jnp.where(past, bs, -jnp.inf)
    sel = jnp.zeros((tq, LANES), jnp.bool_)
    for _ in range(MOBA_TOPK):
        m = jnp.max(bsm, axis=1, keepdims=True)
        p = jnp.min(jnp.where(bsm == m, lane, 4 * LANES), axis=1, keepdims=True)
        hit = lane == p
        sel = sel | (hit & (m > -jnp.inf))
        bsm = jnp.where(hit, -jnp.inf, bsm)

    ti = lax.broadcasted_iota(jnp.int32, (tq, LANES), 0).astype(F32)
    qt = -slope * ti
    qt_hi = qt.astype(BF16).astype(F32)
    qt_lo = qt - qt_hi
    qfeat = jnp.where(past & jnp.logical_not(sel), NEG_BIG, 0.0)
    qfeat = jnp.where(lane == LANE_QT_HI, qt_hi, qfeat)
    qfeat = jnp.where(lane == LANE_QT_LO, qt_lo, qfeat)
    qfeat = jnp.where((lane == LANE_KS_HI) | (lane == LANE_KS_LO), 1.0, qfeat)
    qaug = jnp.where(lane < HEAD_DIM, qf, qfeat).astype(BF16)

    def kv_slice(j):
        return pl.ds(pl.multiple_of(j * blk, blk), blk)

    def scores(j):
        return lax.dot_general(qaug, kaug_ref[kv_slice(j), :], NT_DIMS, preferred_element_type=F32)

    s0 = scores(i)
    r = lax.broadcasted_iota(jnp.int32, (tq, blk), 0)
    c = lax.broadcasted_iota(jnp.int32, (tq, blk), 1)
    s0 = jnp.where(c <= r, s0, NEG_BIG)
    m0 = jnp.max(s0, axis=1, keepdims=True)
    p0 = jnp.exp(s0 - m0)
    l0 = jnp.sum(p0, axis=1, keepdims=True)
    acc0 = jnp.dot(p0.astype(BF16), v_ref[0, kv_slice(i), :], preferred_element_type=F32)

    def body(j, carry):
        m, l, acc = carry
        s = scores(j)
        cj = -slope * (blk * (i - j)).astype(F32)
        mn = jnp.maximum(m, jnp.max(s, axis=1, keepdims=True) + cj)
        alpha = jnp.exp(m - mn)
        p = jnp.exp(s - (mn - cj))
        l = alpha * l + jnp.sum(p, axis=1, keepdims=True)
        acc = alpha * acc + jnp.dot(p.astype(BF16), v_ref[0, kv_slice(j), :], preferred_element_type=F32)
        return mn, l, acc

    _, l, acc = lax.fori_loop(0, i, body, (m0, l0, acc0))
    o_ref[0] = (acc / l).astype(BF16)


def _moba(qkv, slopes, B, S):
    T = B * S
    nq = S // MOBA_BLOCK
    nb = S // MOBA_BLOCK
    assert nb <= MAX_KV_BLOCKS
    H = N_HEADS
    return pl.pallas_call(
        functools.partial(_moba_kernel, nb=nb),
        grid=(B * H, nq),
        in_specs=[pl.BlockSpec(memory_space=pltpu.SMEM),
                  pl.BlockSpec((1, MOBA_BLOCK, LANES), lambda bh, i: (bh % H, (bh // H) * nq + i, 0)),
                  pl.BlockSpec((1, S, LANES), lambda bh, i: (H + bh % H, bh // H, 0)),
                  pl.BlockSpec((1, S, LANES), lambda bh, i: (2 * H + bh % H, bh // H, 0))],
        out_specs=pl.BlockSpec((1, MOBA_BLOCK, LANES), lambda bh, i: (bh % H, (bh // H) * nq + i, 0)),
        out_shape=jax.ShapeDtypeStruct((H, T, LANES), BF16),
        scratch_shapes=[pltpu.VMEM((S, LANES), BF16), pltpu.VMEM((LANES, LANES), F32)],
        compiler_params=_cparams(("arbitrary", "arbitrary")),
        name="moba_attn",
    )(slopes, qkv, qkv, qkv)


def _sb_kernel(q_ref, k_ref, v_ref, tri_ref, o_ref):
    i = pl.program_id(1)
    blk = MOBA_BLOCK
    q = q_ref[0]
    tq = q.shape[0]
    tri = tri_ref[...]

    def tile(j, carry, acc, diag):
        sl = pl.ds(pl.multiple_of(j * blk, blk), blk)
        z = lax.dot_general(q, k_ref[0, sl, :], NT_DIMS, preferred_element_type=F32)
        lg = jnp.minimum(-z, 0.0) - jnp.log(1.0 + jnp.exp(-jnp.abs(z)))
        if diag:
            r = lax.broadcasted_iota(jnp.int32, (tq, blk), 0)
            c = lax.broadcasted_iota(jnp.int32, (tq, blk), 1)
            strict = c < r
            lg = jnp.where(strict, lg, 0.0)
        hi, lo = _split_hi_lo(lg)
        cum = (jnp.dot(hi, tri, preferred_element_type=F32)
               + jnp.dot(lo, tri, preferred_element_type=F32))
        w = jnp.exp(z + cum)
        if diag:
            w = jnp.where(strict, w, 0.0)
        pv = jnp.dot(w.astype(BF16), v_ref[0, sl, :], preferred_element_type=F32)
        acc = acc + jnp.exp(carry) * pv
        carry = carry + cum[:, 0:1]
        return carry, acc

    carry0 = jnp.zeros((tq, 1), F32)
    acc0 = jnp.zeros((tq, LANES), F32)
    carry, acc = tile(i, carry0, acc0, True)

    def body(t, ca):
        return tile(i - 1 - t, ca[0], ca[1], False)

    _, acc = lax.fori_loop(0, i, body, (carry, acc))
    o_ref[0] = acc.astype(BF16)


def _sb(qkv, tri, B, S):
    T = B * S
    nq = S // MOBA_BLOCK
    H = N_HEADS
    return pl.pallas_call(
        _sb_kernel,
        grid=(B * H, nq),
        in_specs=[pl.BlockSpec((1, MOBA_BLOCK, LANES), lambda bh, i: (3 * H + bh % H, (bh // H) * nq + i, 0)),
                  pl.BlockSpec((1, S, LANES), lambda bh, i: (4 * H + bh % H, bh // H, 0)),
                  pl.BlockSpec((1, S, LANES), lambda bh, i: (5 * H + bh % H, bh // H, 0)),
                  pl.BlockSpec((MOBA_BLOCK, MOBA_BLOCK), lambda bh, i: (0, 0))],
        out_specs=pl.BlockSpec((1, MOBA_BLOCK, LANES), lambda bh, i: (bh % H, (bh // H) * nq + i, 0)),
        out_shape=jax.ShapeDtypeStruct((H, T, LANES), BF16),
        compiler_params=_cparams(("arbitrary", "arbitrary")),
        name="sb_attn",
    )(qkv, qkv, qkv, tri)


def _mix_kernel(x_ref, oa_ref, ob_ref, gate_ref, woa_ref, wob_ref, wmix_ref, g2_ref, wq_ref,
                h_ref, x2_ref, qp_ref):
    D = x_ref.shape[1]
    ya = jnp.dot(oa_ref[0], woa_ref[0], preferred_element_type=F32)
    yb = jnp.dot(ob_ref[0], wob_ref[0], preferred_element_type=F32)
    for h in range(1, N_HEADS):
        ya = ya + jnp.dot(oa_ref[h], woa_ref[h], preferred_element_type=F32)
        yb = yb + jnp.dot(ob_ref[h], wob_ref[h], preferred_element_type=F32)
    mixed = gate_ref[:, :D].astype(F32) * ya + gate_ref[:, D:].astype(F32) * yb
    h = x_ref[...] + jnp.dot(mixed.astype(BF16), wmix_ref[...], preferred_element_type=F32)
    h_ref[...] = h
    x2 = _rmsnorm(h, g2_ref[...]).astype(BF16)
    x2_ref[...] = x2
    qp_ref[...] = jnp.dot(x2, wq_ref[...], preferred_element_type=F32)


def _mix(x2d, oa, ob, gates, woa, wob, wmix, g2, wq, *, tm):
    T, D = x2d.shape
    nqp = wq.shape[1]
    H = N_HEADS
    const = lambda *shape: pl.BlockSpec(shape, lambda i: (0,) * len(shape))
    return pl.pallas_call(
        _mix_kernel,
        grid=(T // tm,),
        in_specs=[pl.BlockSpec((tm, D), lambda i: (i, 0)),
                  pl.BlockSpec((H, tm, LANES), lambda i: (0, i, 0)),
                  pl.BlockSpec((H, tm, LANES), lambda i: (0, i, 0)),
                  pl.BlockSpec((tm, 2 * D), lambda i: (i, 0)),
                  const(H, LANES, D), const(H, LANES, D), const(D, D), const(1, D), const(D, nqp)],
        out_specs=[pl.BlockSpec((tm, D), lambda i: (i, 0)),
                   pl.BlockSpec((tm, D), lambda i: (i, 0)),
                   pl.BlockSpec((tm, nqp), lambda i: (i, 0))],
        out_shape=[jax.ShapeDtypeStruct((T, D), F32),
                   jax.ShapeDtypeStruct((T, D), BF16),
                   jax.ShapeDtypeStruct((T, nqp), F32)],
        compiler_params=_cparams(("arbitrary",)),
        name="mix_out",
    )(x2d, oa, ob, gates, woa, wob, wmix, g2, wq)


_CAND_SLABS = ((0, 16, 0), (1, 8, 16), (2, 5, 24), (3, 4, 32), (4, 3, 40), (5, 2, 48), (6, 2, 56), (7, 2, 64))
_CAND_TAIL_ROW0 = 72
_CAND_ROWS = 80


def _topk_rank(x, k, top_ref=None):
    R, n = x.shape
    row = lax.broadcasted_iota(jnp.int32, (R, n), 0)
    rank = jnp.full((R, n), k, jnp.int32)
    vals = []
    for r in range(k):
        m = jnp.max(x, axis=0, keepdims=True)
        p = jnp.min(jnp.where(x == m, row, R), axis=0, keepdims=True)
        hit = row == p
        rank = jnp.where(hit, r, rank)
        x = jnp.where(hit, -jnp.inf, x)
        vals.append(m)
        if top_ref is not None:
            top_ref[r:r + 1, :] = m
    return rank, vals


def _route_kernel(qp_ref, sk_ref, lr_ref, e1_ref, r2_ref, e2_ref, top1_ref, top2_ref, cand_ref):
    K = PEER_TOPK
    q = qp_ref[...]
    half = q.shape[1] // 2
    s1 = lax.dot_general(sk_ref[0, 0], q[:, :half], NT_DIMS, precision=lax.Precision.HIGHEST,
                         preferred_element_type=F32)
    s2 = lax.dot_general(sk_ref[0, 1], q[:, half:], NT_DIMS, precision=lax.Precision.HIGHEST,
                         preferred_element_type=F32)
    n = s1.shape[1]
    rank1, _ = _topk_rank(s1, K, top1_ref)
    rank2, _ = _topk_rank(s2, K, top2_ref)
    t1 = top1_ref[...]
    t2 = top2_ref[...]
    row8 = lax.broadcasted_iota(jnp.int32, (8, n), 0)
    for a, nvalid, off in _CAND_SLABS:
        rows = 16 if nvalid == 16 else 8
        slab = t1[a:a + 1, :] + t2[0:rows, :]
        if nvalid < rows:
            slab = jnp.where(row8 < nvalid, slab, -jnp.inf)
        cand_ref[off:off + rows, :] = slab
    cand_ref[_CAND_TAIL_ROW0:_CAND_ROWS, :] = t1[8:16, :] + t2[0:1, :]

    crank, cvals = _topk_rank(cand_ref[...], K)
    sel = (crank < K).astype(F32)
    z = jnp.zeros((1, n), F32)
    for r in range(K):
        z = z + jnp.exp(cvals[r] - cvals[0])
    lr = jnp.zeros(s1.shape, F32)
    for a in range(K):
        if a < 8:
            _, nvalid, off = _CAND_SLABS[a]
            rows = 16 if nvalid == 16 else 8
            cnt = jnp.sum(sel[off:off + rows, :], axis=0, keepdims=True)
        else:
            cnt = sel[_CAND_TAIL_ROW0 + a - 8:_CAND_TAIL_ROW0 + a - 7, :]
        lr = jnp.where(rank1 == a, cnt, lr)
    lr_ref[0] = lr
    e1_ref[0] = jnp.exp(s1 - t1[0:1, :]) / z
    r2_ref[0] = rank2.astype(F32)
    e2_ref[0] = jnp.exp(s2 - t2[0:1, :])


def _route(qp, sub_keys, *, tt):
    T = qp.shape[0]
    PH, NK = PEER_HEADS, PEER_NKEYS
    qd = qp.shape[1] // PH
    out = jax.ShapeDtypeStruct((PH, NK, T), F32)
    ospec = pl.BlockSpec((1, NK, tt), lambda i, h: (h, 0, i))
    return pl.pallas_call(
        _route_kernel,
        grid=(T // tt, PH),
        in_specs=[pl.BlockSpec((tt, qd), lambda i, h: (i, h)),
                  pl.BlockSpec((1, 2, NK, qd // 2), lambda i, h: (h, 0, 0, 0))],
        out_specs=[ospec] * 4,
        out_shape=[out] * 4,
        scratch_shapes=[pltpu.VMEM((PEER_TOPK, tt), F32), pltpu.VMEM((PEER_TOPK, tt), F32),
                        pltpu.VMEM((_CAND_ROWS, tt), F32)],
        compiler_params=_cparams(("arbitrary", "arbitrary")),
        name="peer_route",
    )(qp, sub_keys)


def _peer_kernel(x2_ref, h_ref, lr_ref, e1_ref, r2_ref, e2_ref, u_ref, vt_ref, gf_ref, o_ref, acc_ref,
                 *, n_i, final_norm):
    e = pl.program_id(1)
    NK = PEER_NKEYS

    @pl.when(e == 0)
    def _():
        acc_ref[...] = jnp.zeros_like(acc_ref)

    at = lax.dot_general(u_ref[...], x2_ref[...], NT_DIMS, preferred_element_type=F32)
    ws = []
    for ii in range(n_i):
        ig = e * n_i + ii
        a = at[ii * NK:(ii + 1) * NK, :]
        ga = 0.5 * a * (1.0 + lax.erf(a * INV_SQRT2))
        coef = jnp.zeros_like(a)
        for hh in range(PEER_HEADS):
            cnt = lr_ref[hh, pl.ds(ig, 1), :]
            e1 = e1_ref[hh, pl.ds(ig, 1), :]
            coef = coef + jnp.where(r2_ref[hh] < cnt, e2_ref[hh], 0.0) * e1
        ws.append((coef * ga).astype(BF16))
    w = jnp.concatenate(ws, axis=0)
    acc_ref[...] += jnp.dot(vt_ref[...], w, preferred_element_type=F32)

    @pl.when(e == pl.num_programs(1) - 1)
    def _():
        y = h_ref[...] + acc_ref[...].T
        o_ref[...] = _rmsnorm(y, gf_ref[...]) if final_norm else y


def _peer(x2, h, lr, e1, r2, e2, u, vt, gf, *, tt, eb, final_norm):
    T, D = h.shape
    NE = u.shape[0]
    PH, NK = PEER_HEADS, PEER_NKEYS
    rspec = pl.BlockSpec((PH, NK, tt), lambda i, e: (0, 0, i))
    return pl.pallas_call(
        functools.partial(_peer_kernel, n_i=eb // NK, final_norm=final_norm),
        grid=(T // tt, NE // eb),
        in_specs=[pl.BlockSpec((tt, D), lambda i, e: (i, 0)),
                  pl.BlockSpec((tt, D), lambda i, e: (i, 0)),
                  rspec, rspec, rspec, rspec,
                  pl.BlockSpec((eb, D), lambda i, e: (e, 0)),
                  pl.BlockSpec((D, eb), lambda i, e: (0, e)),
                  pl.BlockSpec((1, D), lambda i, e: (0, 0))],
        out_specs=pl.BlockSpec((tt, D), lambda i, e: (i, 0)),
        out_shape=jax.ShapeDtypeStruct((T, D), F32),
        scratch_shapes=[pltpu.VMEM((D, tt), F32)],
        compiler_params=_cparams(("arbitrary", "arbitrary")),
        name="peer_experts",
    )(x2, h, lr, e1, r2, e2, u, vt, gf)


def _pad_heads_cols(w):
    D, n = w.shape
    w = w.reshape(D, n // HEAD_DIM, HEAD_DIM)
    w = jnp.pad(w, ((0, 0), (0, 0), (0, LANES - HEAD_DIM)))
    return w.reshape(D, (n // HEAD_DIM) * LANES)


def _pad_heads_rows(w):
    n, D = w.shape
    w = w.reshape(n // HEAD_DIM, HEAD_DIM, D)
    return jnp.pad(w, ((0, 0), (0, LANES - HEAD_DIM), (0, 0)))


def kernel(x, norm1_g, w_in, w_out_moba, w_out_sb, w_mix_out, norm2_g, peer_w_q, peer_sub_keys, peer_u,
           peer_v, final_norm_g):
    B, S, D = x.shape
    T = B * S
    depth = w_in.shape[0]
    n_qkv = 6 * N_HEADS * HEAD_DIM
    tm = min(1024, T)
    slopes = jnp.tile(jnp.exp2(-8.0 * jnp.arange(1, N_HEADS + 1, dtype=F32) / N_HEADS), B)
    tri = jnp.tril(jnp.ones((MOBA_BLOCK, MOBA_BLOCK), F32)).astype(BF16)

    h = x.reshape(T, D)
    out = h
    for l in range(depth):
        w_heads = _pad_heads_cols(w_in[l][:, :n_qkv]).astype(BF16)
        w_gates = w_in[l][:, n_qkv:].astype(BF16)
        g1 = norm1_g[l].reshape(1, D)
        qkv = _proj(h, g1, w_heads, per_head=True, tm=tm)
        gates = _proj(h, g1, w_gates, per_head=False, tm=tm)
        oa = _moba(qkv, slopes, B, S)
        ob = _sb(qkv, tri, B, S)
        h, x2, qp = _mix(h, oa, ob, gates,
                         _pad_heads_rows(w_out_moba[l]).astype(BF16),
                         _pad_heads_rows(w_out_sb[l]).astype(BF16),
                         w_mix_out[l].astype(BF16), norm2_g[l].reshape(1, D),
                         peer_w_q[l].astype(BF16), tm=min(256, T))
        lr, e1, r2, e2 = _route(qp, peer_sub_keys[l], tt=min(256, T))
        out = _peer(x2, h, lr, e1, r2, e2, peer_u[l].astype(BF16), peer_v[l].T.astype(BF16),
                    final_norm_g.reshape(1, D), tt=min(512, T), eb=512, final_norm=(l == depth - 1))
        h = out
    return out.reshape(B, S, D)
```

```python
import functools

import jax
import jax.numpy as jnp
from jax import lax
from jax.experimental import pallas as pl
from jax.experimental.pallas import tpu as pltpu

F32 = jnp.float32
BF16 = jnp.bfloat16

HEAD_DIM = 64
N_HEADS = 8
LANES = 128
MOBA_BLOCK = 256
MOBA_TOPK = 3
PEER_HEADS = 8
PEER_NKEYS = 128
PEER_TOPK = 16
RMS_EPS = 1e-6
NEG_BIG = -1e30
INV_SQRT2 = 0.7071067811865476
VMEM_LIMIT = 56 * 1024 * 1024

LANE_BLK0 = 64
MAX_KV_BLOCKS = 32
LANE_QPOS = 96
LANE_KPOS = 99
MOBA_HEADS_PER_STEP = 2
MOBA_KV_CHUNK = 4
SB_HEADS_PER_STEP = 2

NT_DIMS = (((1,), (1,)), ((), ()))


def _cparams(sem):
    return pltpu.CompilerParams(dimension_semantics=sem, vmem_limit_bytes=VMEM_LIMIT)


def _rmsnorm(x, g):
    return x * lax.rsqrt(jnp.mean(x * x, axis=-1, keepdims=True) + RMS_EPS) * g


def _split_hi_lo(x):
    hi = x.astype(BF16)
    lo = (x - hi.astype(F32)).astype(BF16)
    return hi, lo


def _split3(x):
    a = x.astype(BF16).astype(F32)
    r = x - a
    b = r.astype(BF16).astype(F32)
    c = (r - b).astype(BF16).astype(F32)
    return a, b, c


def _proj_kernel(x_ref, g_ref, w_ref, o_ref, xn_ref, *, per_head):
    j = pl.program_id(1)

    @pl.when(j == 0)
    def _():
        xn_ref[...] = _rmsnorm(x_ref[...], g_ref[...]).astype(BF16)

    res = jnp.dot(xn_ref[...], w_ref[...], preferred_element_type=F32)
    if per_head:
        res = res * jnp.where((j == 0) | (j == 3), HEAD_DIM ** -0.5, 1.0)
        for h in range(N_HEADS):
            o_ref[h] = res[:, h * LANES:(h + 1) * LANES].astype(BF16)
    else:
        o_ref[...] = jax.nn.sigmoid(res).astype(BF16)


def _proj(x2d, g, w, *, per_head, tm):
    T, D = x2d.shape
    ncol = w.shape[1] // 1024
    if per_head:
        out_shape = jax.ShapeDtypeStruct((ncol * N_HEADS, T, LANES), BF16)
        out_spec = pl.BlockSpec((N_HEADS, tm, LANES), lambda i, j: (j, i, 0))
    else:
        out_shape = jax.ShapeDtypeStruct((T, w.shape[1]), BF16)
        out_spec = pl.BlockSpec((tm, 1024), lambda i, j: (i, j))
    return pl.pallas_call(
        functools.partial(_proj_kernel, per_head=per_head),
        grid=(T // tm, ncol),
        in_specs=[pl.BlockSpec((tm, D), lambda i, j: (i, 0)),
                  pl.BlockSpec((1, D), lambda i, j: (0, 0)),
                  pl.BlockSpec((D, 1024), lambda i, j: (0, j))],
        out_specs=out_spec,
        out_shape=out_shape,
        scratch_shapes=[pltpu.VMEM((tm, D), BF16)],
        compiler_params=_cparams(("arbitrary", "arbitrary")),
        name="proj_heads" if per_head else "proj_gates",
    )(x2d, g, w)


def _moba_kernel(slope_ref, q_ref, k_ref, v_ref, o_ref, kaug_ref, vaug_ref, kmean_ref, *, nb, groups):
    g = pl.program_id(0)
    i = pl.program_id(1)
    hg, tq, _ = q_ref.shape
    blk = MOBA_BLOCK
    head0 = (g % groups) * hg

    @pl.when(i == 0)
    def _():
        kmean_ref[...] = jnp.zeros_like(kmean_ref)
        row = lax.broadcasted_iota(jnp.int32, (blk, LANES), 0)
        lane = lax.broadcasted_iota(jnp.int32, (blk, LANES), 1)
        ones_q = jnp.where((lane >= LANE_QPOS) & (lane < LANE_QPOS + 3), 1.0, 0.0)
        for hh in range(hg):
            slope = slope_ref[head0 + hh]

            def fill(n, c, hh=hh, slope=slope):
                sl = pl.ds(pl.multiple_of(n * blk, blk), blk)
                kf = k_ref[hh, sl, :].astype(F32)
                kmean_ref[hh, pl.ds(n, 1), :] = jnp.mean(kf, axis=0, keepdims=True)
                pa, pb, pc = _split3(slope * (n * blk + row).astype(F32))
                feat = jnp.where(lane == LANE_BLK0 + n, 1.0, ones_q)
                feat = jnp.where(lane == LANE_KPOS, pa, feat)
                feat = jnp.where(lane == LANE_KPOS + 1, pb, feat)
                feat = jnp.where(lane == LANE_KPOS + 2, pc, feat)
                kaug_ref[hh, sl, :] = jnp.where(lane < HEAD_DIM, kf, feat).astype(BF16)
                vf = v_ref[hh, sl, :].astype(F32)
                vaug_ref[hh, sl, :] = jnp.where(lane == HEAD_DIM, 1.0, vf).astype(BF16)
                return c

            lax.fori_loop(0, nb, fill, 0)

    lane = lax.broadcasted_iota(jnp.int32, (tq, LANES), 1)
    is_blk_lane = (lane >= LANE_BLK0) & (lane < LANE_BLK0 + MAX_KV_BLOCKS)
    rowb = lax.broadcasted_iota(jnp.int32, (MAX_KV_BLOCKS, tq), 0)
    r8 = lax.broadcasted_iota(jnp.int32, (8, tq), 0)
    tpos = (i * blk + lax.broadcasted_iota(jnp.int32, (8, tq), 1)).astype(F32)
    own_sl = pl.ds(pl.multiple_of(i * blk, blk), blk)
    r = lax.broadcasted_iota(jnp.int32, (tq, blk), 0)
    c = lax.broadcasted_iota(jnp.int32, (tq, blk), 1)
    causal = c <= r

    qaugs, states = [], []
    for hh in range(hg):
        slope = slope_ref[head0 + hh]
        qf = q_ref[hh].astype(F32)
        bs = lax.dot_general(kmean_ref[hh], qf, NT_DIMS, precision=lax.Precision.HIGHEST,
                             preferred_element_type=F32)
        bsm = jnp.where(rowb < i, bs, -jnp.inf)
        sel = jnp.zeros(bs.shape, jnp.bool_)
        for _ in range(MOBA_TOPK):
            m = jnp.max(bsm, axis=0, keepdims=True)
            p = jnp.min(jnp.where(bsm == m, rowb, MAX_KV_BLOCKS), axis=0, keepdims=True)
            hit = rowb == p
            sel = sel | (hit & (m > -jnp.inf))
            bsm = jnp.where(hit, -jnp.inf, bsm)
        bias_t = jnp.where(sel, 0.0, NEG_BIG)
        pa, pb, pc = _split3(-slope * tpos)
        pos_t = jnp.where(r8 == 0, pa, jnp.where(r8 == 1, pb, jnp.where(r8 == 2, pc,
                                                                       jnp.where(r8 < 6, 1.0, 0.0))))
        qfeat = jnp.concatenate([jnp.zeros((LANE_BLK0, tq), F32), bias_t, pos_t,
                                 jnp.zeros((LANES - LANE_QPOS - 8, tq), F32)], axis=0).T
        qaug = jnp.where(lane < HEAD_DIM, qf, qfeat).astype(BF16)
        qaugs.append(qaug)

        qown = jnp.where(is_blk_lane, 0.0, qaug.astype(F32)).astype(BF16)
        s0 = lax.dot_general(qown, kaug_ref[hh, own_sl, :], NT_DIMS, preferred_element_type=F32)
        s0 = jnp.where(causal, s0, NEG_BIG)
        m0 = jnp.max(s0, axis=1, keepdims=True)
        p0 = jnp.exp(s0 - m0)
        acc0 = jnp.dot(p0.astype(BF16), vaug_ref[hh, own_sl, :], preferred_element_type=F32)
        states.append((m0, acc0))

    span = MOBA_KV_CHUNK * blk

    def body(cidx, states):
        sl = pl.ds(pl.multiple_of(cidx * span, span), span)
        new = []
        for hh in range(hg):
            m, acc = states[hh]
            s = lax.dot_general(qaugs[hh], kaug_ref[hh, sl, :], NT_DIMS, preferred_element_type=F32)
            mn = jnp.maximum(m, jnp.max(s, axis=1, keepdims=True))
            alpha = jnp.exp(m - mn)
            p = jnp.exp(s - mn)
            acc = alpha * acc + jnp.dot(p.astype(BF16), vaug_ref[hh, sl, :], preferred_element_type=F32)
            new.append((mn, acc))
        return tuple(new)

    n_chunks = (i + MOBA_KV_CHUNK - 1) // MOBA_KV_CHUNK
    states = lax.fori_loop(0, n_chunks, body, tuple(states))
    for hh in range(hg):
        acc = states[hh][1]
        denom = acc[:, HEAD_DIM:HEAD_DIM + 1]
        o_ref[hh] = jnp.where(lane < HEAD_DIM, acc / denom, 0.0).astype(BF16)


def _moba(qkv, slopes, B, S):
    T = B * S
    nq = S // MOBA_BLOCK
    nb = S // MOBA_BLOCK
    assert nb <= MAX_KV_BLOCKS and nb % MOBA_KV_CHUNK == 0
    H = N_HEADS
    hg = MOBA_HEADS_PER_STEP
    G = H // hg
    return pl.pallas_call(
        functools.partial(_moba_kernel, nb=nb, groups=G),
        grid=(B * G, nq),
        in_specs=[pl.BlockSpec(memory_space=pltpu.SMEM),
                  pl.BlockSpec((hg, MOBA_BLOCK, LANES), lambda g, i: (g % G, (g // G) * nq + i, 0)),
                  pl.BlockSpec((hg, S, LANES), lambda g, i: (G + g % G, g // G, 0)),
                  pl.BlockSpec((hg, S, LANES), lambda g, i: (2 * G + g % G, g // G, 0))],
        out_specs=pl.BlockSpec((hg, MOBA_BLOCK, LANES), lambda g, i: (g % G, (g // G) * nq + i, 0)),
        out_shape=jax.ShapeDtypeStruct((H, T, LANES), BF16),
        scratch_shapes=[pltpu.VMEM((hg, S, LANES), BF16), pltpu.VMEM((hg, S, LANES), BF16),
                        pltpu.VMEM((hg, MAX_KV_BLOCKS, LANES), F32)],
        compiler_params=_cparams(("arbitrary", "arbitrary")),
        name="moba_attn",
    )(slopes, qkv, qkv, qkv)


def _sb_kernel(q_ref, k_ref, v_ref, tri_ref, o_ref):
    i = pl.program_id(1)
    blk = MOBA_BLOCK
    hg, tq, _ = q_ref.shape
    tri = tri_ref[...]

    def tile(hh, j, state, diag):
        carry, decay, acc = state
        sl = pl.ds(pl.multiple_of(j * blk, blk), blk)
        z = lax.dot_general(q_ref[hh], k_ref[hh, sl, :], NT_DIMS, preferred_element_type=F32)
        lg = jnp.minimum(-z, 0.0) - jnp.log(1.0 + jnp.exp(-jnp.abs(z)))
        if diag:
            r = lax.broadcasted_iota(jnp.int32, (tq, blk), 0)
            c = lax.broadcasted_iota(jnp.int32, (tq, blk), 1)
            strict = c < r
            lg = jnp.where(strict, lg, 0.0)
        hi, lo = _split_hi_lo(lg)
        cum = (jnp.dot(hi, tri, preferred_element_type=F32)
               + jnp.dot(lo, tri, preferred_element_type=F32))
        w = jnp.exp(z + cum)
        if diag:
            w = jnp.where(strict, w, 0.0)
        pv = jnp.dot(w.astype(BF16), v_ref[hh, sl, :], preferred_element_type=F32)
        acc = acc + decay * pv
        carry = carry + cum[:, 0:1]
        return carry, jnp.exp(carry), acc

    def any_alive(states):
        alive = jnp.max(states[0][1])
        for st in states[1:]:
            alive = jnp.maximum(alive, jnp.max(st[1]))
        return alive > 0.0

    init = (jnp.zeros((tq, 1), F32), jnp.ones((tq, 1), F32), jnp.zeros((tq, LANES), F32))
    states = tuple(tile(hh, i, init, True) for hh in range(hg))

    def cond(c):
        t, alive, _ = c
        return (t < i) & alive

    def body(c):
        t, _, states = c
        states = tuple(tile(hh, i - 1 - t, states[hh], False) for hh in range(hg))
        return t + 1, any_alive(states), states

    _, _, states = lax.while_loop(cond, body, (jnp.int32(0), any_alive(states), states))
    for hh in range(hg):
        o_ref[hh] = states[hh][2].astype(BF16)


def _sb(qkv, tri, B, S):
    T = B * S
    nq = S // MOBA_BLOCK
    H = N_HEADS
    hg = SB_HEADS_PER_STEP
    G = H // hg
    return pl.pallas_call(
        _sb_kernel,
        grid=(B * G, nq),
        in_specs=[pl.BlockSpec((hg, MOBA_BLOCK, LANES), lambda g, i: (3 * G + g % G, (g // G) * nq + i, 0)),
                  pl.BlockSpec((hg, S, LANES), lambda g, i: (4 * G + g % G, g // G, 0)),
                  pl.BlockSpec((hg, S, LANES), lambda g, i: (5 * G + g % G, g // G, 0)),
                  pl.BlockSpec((MOBA_BLOCK, MOBA_BLOCK), lambda g, i: (0, 0))],
        out_specs=pl.BlockSpec((hg, MOBA_BLOCK, LANES), lambda g, i: (g % G, (g // G) * nq + i, 0)),
        out_shape=jax.ShapeDtypeStruct((H, T, LANES), BF16),
        compiler_params=_cparams(("arbitrary", "arbitrary")),
        name="sb_attn",
    )(qkv, qkv, qkv, tri)


def _mix_kernel(x_ref, oa_ref, ob_ref, gate_ref, woa_ref, wob_ref, wmix_ref, g2_ref, wq_ref,
                h_ref, x2_ref, qp_ref):
    D = x_ref.shape[1]
    ya = jnp.dot(oa_ref[0], woa_ref[0], preferred_element_type=F32)
    yb = jnp.dot(ob_ref[0], wob_ref[0], preferred_element_type=F32)
    for h in range(1, N_HEADS):
        ya = ya + jnp.dot(oa_ref[h], woa_ref[h], preferred_element_type=F32)
        yb = yb + jnp.dot(ob_ref[h], wob_ref[h], preferred_element_type=F32)
    mixed = gate_ref[:, :D].astype(F32) * ya + gate_ref[:, D:].astype(F32) * yb
    h = x_ref[...] + jnp.dot(mixed.astype(BF16), wmix_ref[...], preferred_element_type=F32)
    h_ref[...] = h
    x2 = _rmsnorm(h, g2_ref[...]).astype(BF16)
    x2_ref[...] = x2
    qp_ref[...] = jnp.dot(x2, wq_ref[...], preferred_element_type=F32)


def _mix(x2d, oa, ob, gates, woa, wob, wmix, g2, wq, *, tm):
    T, D = x2d.shape
    nqp = wq.shape[1]
    H = N_HEADS
    const = lambda *shape: pl.BlockSpec(shape, lambda i: (0,) * len(shape))
    return pl.pallas_call(
        _mix_kernel,
        grid=(T // tm,),
        in_specs=[pl.BlockSpec((tm, D), lambda i: (i, 0)),
                  pl.BlockSpec((H, tm, LANES), lambda i: (0, i, 0)),
                  pl.BlockSpec((H, tm, LANES), lambda i: (0, i, 0)),
                  pl.BlockSpec((tm, 2 * D), lambda i: (i, 0)),
                  const(H, LANES, D), const(H, LANES, D), const(D, D), const(1, D), const(D, nqp)],
        out_specs=[pl.BlockSpec((tm, D), lambda i: (i, 0)),
                   pl.BlockSpec((tm, D), lambda i: (i, 0)),
                   pl.BlockSpec((tm, nqp), lambda i: (i, 0))],
        out_shape=[jax.ShapeDtypeStruct((T, D), F32),
                   jax.ShapeDtypeStruct((T, D), BF16),
                   jax.ShapeDtypeStruct((T, nqp), F32)],
        compiler_params=_cparams(("arbitrary",)),
        name="mix_out",
    )(x2d, oa, ob, gates, woa, wob, wmix, g2, wq)


_CAND_SLABS = ((0, 16, 0), (1, 8, 16), (2, 5, 24), (3, 4, 32), (4, 3, 40), (5, 2, 48), (6, 2, 56), (7, 2, 64))
_CAND_TAIL_ROW0 = 72
_CAND_ROWS = 80


def _topk_rank(x, k, top_ref=None):
    R, n = x.shape
    row = lax.broadcasted_iota(jnp.int32, (R, n), 0)
    rank = jnp.full((R, n), k, jnp.int32)
    vals = []
    for r in range(k):
        m = jnp.max(x, axis=0, keepdims=True)
        p = jnp.min(jnp.where(x == m, row, R), axis=0, keepdims=True)
        hit = row == p
        rank = jnp.where(hit, r, rank)
        x = jnp.where(hit, -jnp.inf, x)
        vals.append(m)
        if top_ref is not None:
            top_ref[r:r + 1, :] = m
    return rank, vals


def _route_kernel(qp_ref, sk_ref, lr_ref, e1_ref, r2_ref, e2_ref, top1_ref, top2_ref, cand_ref):
    K = PEER_TOPK
    q = qp_ref[...]
    half = q.shape[1] // 2
    s1 = lax.dot_general(sk_ref[0, 0], q[:, :half], NT_DIMS, precision=lax.Precision.HIGHEST,
                         preferred_element_type=F32)
    s2 = lax.dot_general(sk_ref[0, 1], q[:, half:], NT_DIMS, precision=lax.Precision.HIGHEST,
                         preferred_element_type=F32)
    n = s1.shape[1]
    rank1, _ = _topk_rank(s1, K, top1_ref)
    rank2, _ = _topk_rank(s2, K, top2_ref)
    t1 = top1_ref[...]
    t2 = top2_ref[...]
    row8 = lax.broadcasted_iota(jnp.int32, (8, n), 0)
    for a, nvalid, off in _CAND_SLABS:
        rows = 16 if nvalid == 16 else 8
        slab = t1[a:a + 1, :] + t2[0:rows, :]
        if nvalid < rows:
            slab = jnp.where(row8 < nvalid, slab, -jnp.inf)
        cand_ref[off:off + rows, :] = slab
    cand_ref[_CAND_TAIL_ROW0:_CAND_ROWS, :] = t1[8:16, :] + t2[0:1, :]

    crank, cvals = _topk_rank(cand_ref[...], K)
    sel = (crank < K).astype(F32)
    z = jnp.zeros((1, n), F32)
    for r in range(K):
        z = z + jnp.exp(cvals[r] - cvals[0])
    lr = jnp.zeros(s1.shape, F32)
    for a in range(K):
        if a < 8:
            _, nvalid, off = _CAND_SLABS[a]
            rows = 16 if nvalid == 16 else 8
            cnt = jnp.sum(sel[off:off + rows, :], axis=0, keepdims=True)
        else:
            cnt = sel[_CAND_TAIL_ROW0 + a - 8:_CAND_TAIL_ROW0 + a - 7, :]
        lr = jnp.where(rank1 == a, cnt, lr)
    lr_ref[0] = lr
    e1_ref[0] = jnp.exp(s1 - t1[0:1, :]) / z
    r2_ref[0] = rank2.astype(F32)
    e2_ref[0] = jnp.exp(s2 - t2[0:1, :])


def _route(qp, sub_keys, *, tt):
    T = qp.shape[0]
    PH, NK = PEER_HEADS, PEER_NKEYS
    qd = qp.shape[1] // PH
    out = jax.ShapeDtypeStruct((PH, NK, T), F32)
    ospec = pl.BlockSpec((1, NK, tt), lambda i, h: (h, 0, i))
    return pl.pallas_call(
        _route_kernel,
        grid=(T // tt, PH),
        in_specs=[pl.BlockSpec((tt, qd), lambda i, h: (i, h)),
                  pl.BlockSpec((1, 2, NK, qd // 2), lambda i, h: (h, 0, 0, 0))],
        out_specs=[ospec] * 4,
        out_shape=[out] * 4,
        scratch_shapes=[pltpu.VMEM((PEER_TOPK, tt), F32), pltpu.VMEM((PEER_TOPK, tt), F32),
                        pltpu.VMEM((_CAND_ROWS, tt), F32)],
        compiler_params=_cparams(("arbitrary", "arbitrary")),
        name="peer_route",
    )(qp, sub_keys)


def _peer_kernel(x2_ref, h_ref, lr_ref, e1_ref, r2_ref, e2_ref, u_ref, vt_ref, gf_ref, o_ref, acc_ref,
                 *, n_i, final_norm):
    e = pl.program_id(1)
    NK = PEER_NKEYS

    @pl.when(e == 0)
    def _():
        acc_ref[...] = jnp.zeros_like(acc_ref)

    at = lax.dot_general(u_ref[...], x2_ref[...], NT_DIMS, preferred_element_type=F32)
    ws = []
    for ii in range(n_i):
        ig = e * n_i + ii
        a = at[ii * NK:(ii + 1) * NK, :]
        ga = 0.5 * a * (1.0 + lax.erf(a * INV_SQRT2))
        coef = jnp.zeros_like(a)
        for hh in range(PEER_HEADS):
            cnt = lr_ref[hh, pl.ds(ig, 1), :]
            e1 = e1_ref[hh, pl.ds(ig, 1), :]
            coef = coef + jnp.where(r2_ref[hh] < cnt, e2_ref[hh], 0.0) * e1
        ws.append((coef * ga).astype(BF16))
    w = jnp.concatenate(ws, axis=0)
    acc_ref[...] += jnp.dot(vt_ref[...], w, preferred_element_type=F32)

    @pl.when(e == pl.num_programs(1) - 1)
    def _():
        y = h_ref[...] + acc_ref[...].T
        o_ref[...] = _rmsnorm(y, gf_ref[...]) if final_norm else y


def _peer(x2, h, lr, e1, r2, e2, u, vt, gf, *, tt, eb, final_norm):
    T, D = h.shape
    NE = u.shape[0]
    PH, NK = PEER_HEADS, PEER_NKEYS
    rspec = pl.BlockSpec((PH, NK, tt), lambda i, e: (0, 0, i))
    return pl.pallas_call(
        functools.partial(_peer_kernel, n_i=eb // NK, final_norm=final_norm),
        grid=(T // tt, NE // eb),
        in_specs=[pl.BlockSpec((tt, D), lambda i, e: (i, 0)),
                  pl.BlockSpec((tt, D), lambda i, e: (i, 0)),
                  rspec, rspec, rspec, rspec,
                  pl.BlockSpec((eb, D), lambda i, e: (e, 0)),
                  pl.BlockSpec((D, eb), lambda i, e: (0, e)),
                  pl.BlockSpec((1, D), lambda i, e: (0, 0))],
        out_specs=pl.BlockSpec((tt, D), lambda i, e: (i, 0)),
        out_shape=jax.ShapeDtypeStruct((T, D), F32),
        scratch_shapes=[pltpu.VMEM((D, tt), F32)],
        compiler_params=_cparams(("arbitrary", "arbitrary")),
        name="peer_experts",
    )(x2, h, lr, e1, r2, e2, u, vt, gf)


def _pad_heads_cols(w):
    D, n = w.shape
    w = w.reshape(D, n // HEAD_DIM, HEAD_DIM)
    w = jnp.pad(w, ((0, 0), (0, 0), (0, LANES - HEAD_DIM)))
    return w.reshape(D, (n // HEAD_DIM) * LANES)


def _pad_heads_rows(w):
    n, D = w.shape
    w = w.reshape(n // HEAD_DIM, HEAD_DIM, D)
    return jnp.pad(w, ((0, 0), (0, LANES - HEAD_DIM), (0, 0)))


def kernel(x, norm1_g, w_in, w_out_moba, w_out_sb, w_mix_out, norm2_g, peer_w_q, peer_sub_keys, peer_u,
           peer_v, final_norm_g):
    B, S, D = x.shape
    T = B * S
    depth = w_in.shape[0]
    n_qkv = 6 * N_HEADS * HEAD_DIM
    tm = min(1024, T)
    slopes = jnp.exp2(-8.0 * jnp.arange(1, N_HEADS + 1, dtype=F32) / N_HEADS)
    tri = jnp.tril(jnp.ones((MOBA_BLOCK, MOBA_BLOCK), F32)).astype(BF16)

    h = x.reshape(T, D)
    out = h
    for l in range(depth):
        w_heads = _pad_heads_cols(w_in[l][:, :n_qkv]).astype(BF16)
        w_gates = w_in[l][:, n_qkv:].astype(BF16)
        g1 = norm1_g[l].reshape(1, D)
        qkv = _proj(h, g1, w_heads, per_head=True, tm=tm)
        gates = _proj(h, g1, w_gates, per_head=False, tm=tm)
        oa = _moba(qkv, slopes, B, S)
        ob = _sb(qkv, tri, B, S)
        h, x2, qp = _mix(h, oa, ob, gates,
                         _pad_heads_rows(w_out_moba[l]).astype(BF16),
                         _pad_heads_rows(w_out_sb[l]).astype(BF16),
                         w_mix_out[l].astype(BF16), norm2_g[l].reshape(1, D),
                         peer_w_q[l].astype(BF16), tm=min(256, T))
        lr, e1, r2, e2 = _route(qp, peer_sub_keys[l], tt=min(256, T))
        out = _peer(x2, h, lr, e1, r2, e2, peer_u[l].astype(BF16), peer_v[l].T.astype(BF16),
                    final_norm_g.reshape(1, D), tt=min(512, T), eb=512, final_norm=(l == depth - 1))
        h = out
    return out.reshape(B, S, D)
```

```python
import functools

import jax
import jax.numpy as jnp
from jax import lax
from jax.experimental import pallas as pl
from jax.experimental.pallas import tpu as pltpu

F32 = jnp.float32
BF16 = jnp.bfloat16

HEAD_DIM = 64
N_HEADS = 8
LANES = 128
MOBA_BLOCK = 256
MOBA_TOPK = 3
PEER_HEADS = 8
PEER_NKEYS = 128
PEER_TOPK = 16
RMS_EPS = 1e-6
NEG_BIG = -1e30
INV_SQRT2 = 0.7071067811865476
VMEM_LIMIT = 56 * 1024 * 1024

LANE_BLK0 = 64
MAX_KV_BLOCKS = 32
LANE_QPOS = 96
LANE_KPOS = 99
MOBA_HEADS_PER_STEP = 2
MOBA_KV_CHUNK = 4
SB_HEADS_PER_STEP = 2

NT_DIMS = (((1,), (1,)), ((), ()))


def _cparams(sem):
    return pltpu.CompilerParams(dimension_semantics=sem, vmem_limit_bytes=VMEM_LIMIT)


def _rmsnorm(x, g):
    return x * lax.rsqrt(jnp.mean(x * x, axis=-1, keepdims=True) + RMS_EPS) * g


def _split_hi_lo(x):
    hi = x.astype(BF16)
    lo = (x - hi.astype(F32)).astype(BF16)
    return hi, lo


def _split3(x):
    a = x.astype(BF16).astype(F32)
    r = x - a
    b = r.astype(BF16).astype(F32)
    c = (r - b).astype(BF16).astype(F32)
    return a, b, c


def _proj_kernel(x_ref, g_ref, w_ref, o_ref, xn_ref, *, per_head):
    j = pl.program_id(1)

    @pl.when(j == 0)
    def _():
        xn_ref[...] = _rmsnorm(x_ref[...], g_ref[...]).astype(BF16)

    res = jnp.dot(xn_ref[...], w_ref[...], preferred_element_type=F32)
    if per_head:
        res = res * jnp.where((j == 0) | (j == 3), HEAD_DIM ** -0.5, 1.0)
        for h in range(N_HEADS):
            o_ref[h] = res[:, h * LANES:(h + 1) * LANES].astype(BF16)
    else:
        o_ref[...] = jax.nn.sigmoid(res).astype(BF16)


def _proj(x2d, g, w, *, per_head, tm):
    T, D = x2d.shape
    ncol = w.shape[1] // 1024
    if per_head:
        out_shape = jax.ShapeDtypeStruct((ncol * N_HEADS, T, LANES), BF16)
        out_spec = pl.BlockSpec((N_HEADS, tm, LANES), lambda i, j: (j, i, 0))
    else:
        out_shape = jax.ShapeDtypeStruct((T, w.shape[1]), BF16)
        out_spec = pl.BlockSpec((tm, 1024), lambda i, j: (i, j))
    return pl.pallas_call(
        functools.partial(_proj_kernel, per_head=per_head),
        grid=(T // tm, ncol),
        in_specs=[pl.BlockSpec((tm, D), lambda i, j: (i, 0)),
                  pl.BlockSpec((1, D), lambda i, j: (0, 0)),
                  pl.BlockSpec((D, 1024), lambda i, j: (0, j))],
        out_specs=out_spec,
        out_shape=out_shape,
        scratch_shapes=[pltpu.VMEM((tm, D), BF16)],
        compiler_params=_cparams(("arbitrary", "arbitrary")),
        name="proj_heads" if per_head else "proj_gates",
    )(x2d, g, w)


def _moba_kernel(slope_ref, q_ref, k_ref, v_ref, o_ref, kaug_ref, vaug_ref, kmean_ref, *, nb, groups):
    g = pl.program_id(0)
    i = pl.program_id(1)
    hg, tq, _ = q_ref.shape
    blk = MOBA_BLOCK
    head0 = (g % groups) * hg

    @pl.when(i == 0)
    def _():
        kmean_ref[...] = jnp.zeros_like(kmean_ref)
        row = lax.broadcasted_iota(jnp.int32, (blk, LANES), 0)
        lane = lax.broadcasted_iota(jnp.int32, (blk, LANES), 1)
        ones_q = jnp.where((lane >= LANE_QPOS) & (lane < LANE_QPOS + 3), 1.0, 0.0)
        for hh in range(hg):
            slope = slope_ref[head0 + hh]

            def fill(n, c, hh=hh, slope=slope):
                sl = pl.ds(pl.multiple_of(n * blk, blk), blk)
                kf = k_ref[hh, sl, :].astype(F32)
                kmean_ref[hh, pl.ds(n, 1), :] = jnp.mean(kf, axis=0, keepdims=True)
                pa, pb, pc = _split3(slope * (n * blk + row).astype(F32))
                feat = jnp.where(lane == LANE_BLK0 + n, 1.0, ones_q)
                feat = jnp.where(lane == LANE_KPOS, pa, feat)
                feat = jnp.where(lane == LANE_KPOS + 1, pb, feat)
                feat = jnp.where(lane == LANE_KPOS + 2, pc, feat)
                kaug_ref[hh, sl, :] = jnp.where(lane < HEAD_DIM, kf, feat).astype(BF16)
                vf = v_ref[hh, sl, :].astype(F32)
                vaug_ref[hh, sl, :] = jnp.where(lane == HEAD_DIM, 1.0, vf).astype(BF16)
                return c

            lax.fori_loop(0, nb, fill, 0)

    lane = lax.broadcasted_iota(jnp.int32, (tq, LANES), 1)
    is_blk_lane = (lane >= LANE_BLK0) & (lane < LANE_BLK0 + MAX_KV_BLOCKS)
    rowb = lax.broadcasted_iota(jnp.int32, (MAX_KV_BLOCKS, tq), 0)
    r8 = lax.broadcasted_iota(jnp.int32, (8, tq), 0)
    tpos = (i * blk + lax.broadcasted_iota(jnp.int32, (8, tq), 1)).astype(F32)
    own_sl = pl.ds(pl.multiple_of(i * blk, blk), blk)
    r = lax.broadcasted_iota(jnp.int32, (tq, blk), 0)
    c = lax.broadcasted_iota(jnp.int32, (tq, blk), 1)
    causal = c <= r

    qaugs, states = [], []
    for hh in range(hg):
        slope = slope_ref[head0 + hh]
        qf = q_ref[hh].astype(F32)
        bs = lax.dot_general(kmean_ref[hh], qf, NT_DIMS, precision=lax.Precision.HIGHEST,
                             preferred_element_type=F32)
        bsm = jnp.where(rowb < i, bs, -jnp.inf)
        sel = jnp.zeros(bs.shape, jnp.bool_)
        for _ in range(MOBA_TOPK):
            m = jnp.max(bsm, axis=0, keepdims=True)
            p = jnp.min(jnp.where(bsm == m, rowb, MAX_KV_BLOCKS), axis=0, keepdims=True)
            hit = rowb == p
            sel = sel | (hit & (m > -jnp.inf))
            bsm = jnp.where(hit, -jnp.inf, bsm)
        bias_t = jnp.where(sel, 0.0, NEG_BIG)
        pa, pb, pc = _split3(-slope * tpos)
        pos_t = jnp.where(r8 == 0, pa, jnp.where(r8 == 1, pb, jnp.where(r8 == 2, pc,
                                                                       jnp.where(r8 < 6, 1.0, 0.0))))
        qfeat = jnp.concatenate([jnp.zeros((LANE_BLK0, tq), F32), bias_t, pos_t,
                                 jnp.zeros((LANES - LANE_QPOS - 8, tq), F32)], axis=0).T
        qaug = jnp.where(lane < HEAD_DIM, qf, qfeat).astype(BF16)
        qaugs.append(qaug)

        qown = jnp.where(is_blk_lane, 0.0, qaug.astype(F32)).astype(BF16)
        s0 = lax.dot_general(qown, kaug_ref[hh, own_sl, :], NT_DIMS, preferred_element_type=F32)
        s0 = jnp.where(causal, s0, NEG_BIG)
        m0 = jnp.max(s0, axis=1, keepdims=True)
        p0 = jnp.exp(s0 - m0)
        acc0 = jnp.dot(p0.astype(BF16), vaug_ref[hh, own_sl, :], preferred_element_type=F32)
        states.append((m0, acc0))

    span = MOBA_KV_CHUNK * blk

    def body(cidx, states):
        sl = pl.ds(pl.multiple_of(cidx * span, span), span)
        new = []
        for hh in range(hg):
            m, acc = states[hh]
            s = lax.dot_general(qaugs[hh], kaug_ref[hh, sl, :], NT_DIMS, preferred_element_type=F32)
            mn = jnp.maximum(m, jnp.max(s, axis=1, keepdims=True))
            alpha = jnp.exp(m - mn)
            p = jnp.exp(s - mn)
            acc = alpha * acc + jnp.dot(p.astype(BF16), vaug_ref[hh, sl, :], preferred_element_type=F32)
            new.append((mn, acc))
        return tuple(new)

    n_chunks = (i + MOBA_KV_CHUNK - 1) // MOBA_KV_CHUNK
    states = lax.fori_loop(0, n_chunks, body, tuple(states))
    for hh in range(hg):
        acc = states[hh][1]
        denom = acc[:, HEAD_DIM:HEAD_DIM + 1]
        o_ref[hh] = jnp.where(lane < HEAD_DIM, acc / denom, 0.0).astype(BF16)


def _moba(qkv, slopes, B, S):
    T = B * S
    nq = S // MOBA_BLOCK
    nb = S // MOBA_BLOCK
    assert nb <= MAX_KV_BLOCKS and nb % MOBA_KV_CHUNK == 0
    H = N_HEADS
    hg = MOBA_HEADS_PER_STEP
    G = H // hg
    return pl.pallas_call(
        functools.partial(_moba_kernel, nb=nb, groups=G),
        grid=(B * G, nq),
        in_specs=[pl.BlockSpec(memory_space=pltpu.SMEM),
                  pl.BlockSpec((hg, MOBA_BLOCK, LANES), lambda g, i: (g % G, (g // G) * nq + i, 0)),
                  pl.BlockSpec((hg, S, LANES), lambda g, i: (G + g % G, g // G, 0)),
                  pl.BlockSpec((hg, S, LANES), lambda g, i: (2 * G + g % G, g // G, 0))],
        out_specs=pl.BlockSpec((hg, MOBA_BLOCK, LANES), lambda g, i: (g % G, (g // G) * nq + i, 0)),
        out_shape=jax.ShapeDtypeStruct((H, T, LANES), BF16),
        scratch_shapes=[pltpu.VMEM((hg, S, LANES), BF16), pltpu.VMEM((hg, S, LANES), BF16),
                        pltpu.VMEM((hg, MAX_KV_BLOCKS, LANES), F32)],
        compiler_params=_cparams(("arbitrary", "arbitrary")),
        name="moba_attn",
    )(slopes, qkv, qkv, qkv)


def _sb_kernel(q_ref, k_ref, v_ref, tri_ref, o_ref):
    i = pl.program_id(1)
    blk = MOBA_BLOCK
    hg, tq, _ = q_ref.shape
    tri = tri_ref[...]

    def tile(hh, j, state, diag):
        carry, decay, acc = state
        sl = pl.ds(pl.multiple_of(j * blk, blk), blk)
        z = lax.dot_general(q_ref[hh], k_ref[hh, sl, :], NT_DIMS, preferred_element_type=F32)
        lg = jnp.minimum(-z, 0.0) - jnp.log(1.0 + jnp.exp(-jnp.abs(z)))
        if diag:
            r = lax.broadcasted_iota(jnp.int32, (tq, blk), 0)
            c = lax.broadcasted_iota(jnp.int32, (tq, blk), 1)
            strict = c < r
            lg = jnp.where(strict, lg, 0.0)
        hi, lo = _split_hi_lo(lg)
        cum = (jnp.dot(hi, tri, preferred_element_type=F32)
               + jnp.dot(lo, tri, preferred_element_type=F32))
        w = jnp.exp(z + cum)
        if diag:
            w = jnp.where(strict, w, 0.0)
        pv = jnp.dot(w.astype(BF16), v_ref[hh, sl, :], preferred_element_type=F32)
        acc = acc + decay * pv
        carry = carry + cum[:, 0:1]
        return carry, jnp.exp(carry), acc

    def any_alive(states):
        alive = jnp.max(states[0][1])
        for st in states[1:]:
            alive = jnp.maximum(alive, jnp.max(st[1]))
        return alive > 0.0

    init = (jnp.zeros((tq, 1), F32), jnp.ones((tq, 1), F32), jnp.zeros((tq, LANES), F32))
    states = tuple(tile(hh, i, init, True) for hh in range(hg))

    def cond(c):
        t, alive, _ = c
        return (t < i) & alive

    def body(c):
        t, _, states = c
        states = tuple(tile(hh, i - 1 - t, states[hh], False) for hh in range(hg))
        return t + 1, any_alive(states), states

    _, _, states = lax.while_loop(cond, body, (jnp.int32(0), any_alive(states), states))
    for hh in range(hg):
        o_ref[hh] = states[hh][2].astype(BF16)


def _sb(qkv, tri, B, S):
    T = B * S
    nq = S // MOBA_BLOCK
    H = N_HEADS
    hg = SB_HEADS_PER_STEP
    G = H // hg
    return pl.pallas_call(
        _sb_kernel,
        grid=(B * G, nq),
        in_specs=[pl.BlockSpec((hg, MOBA_BLOCK, LANES), lambda g, i: (3 * G + g % G, (g // G) * nq + i, 0)),
                  pl.BlockSpec((hg, S, LANES), lambda g, i: (4 * G + g % G, g // G, 0)),
                  pl.BlockSpec((hg, S, LANES), lambda g, i: (5 * G + g % G, g // G, 0)),
                  pl.BlockSpec((MOBA_BLOCK, MOBA_BLOCK), lambda g, i: (0, 0))],
        out_specs=pl.BlockSpec((hg, MOBA_BLOCK, LANES), lambda g, i: (g % G, (g // G) * nq + i, 0)),
        out_shape=jax.ShapeDtypeStruct((H, T, LANES), BF16),
        compiler_params=_cparams(("arbitrary", "arbitrary")),
        name="sb_attn",
    )(qkv, qkv, qkv, tri)


def _mix_kernel(x_ref, oa_ref, ob_ref, gate_ref, woa_ref, wob_ref, wmix_ref, g2_ref, wq_ref,
                h_ref, x2_ref, qp_ref):
    D = x_ref.shape[1]
    ya = jnp.dot(oa_ref[0], woa_ref[0], preferred_element_type=F32)
    yb = jnp.dot(ob_ref[0], wob_ref[0], preferred_element_type=F32)
    for h in range(1, N_HEADS):
        ya = ya + jnp.dot(oa_ref[h], woa_ref[h], preferred_element_type=F32)
        yb = yb + jnp.dot(ob_ref[h], wob_ref[h], preferred_element_type=F32)
    mixed = gate_ref[:, :D].astype(F32) * ya + gate_ref[:, D:].astype(F32) * yb
    h = x_ref[...] + jnp.dot(mixed.astype(BF16), wmix_ref[...], preferred_element_type=F32)
    h_ref[...] = h
    x2 = _rmsnorm(h, g2_ref[...]).astype(BF16)
    x2_ref[...] = x2
    qp_ref[...] = jnp.dot(x2, wq_ref[...], preferred_element_type=F32)


def _mix(x2d, oa, ob, gates, woa, wob, wmix, g2, wq, *, tm):
    T, D = x2d.shape
    nqp = wq.shape[1]
    H = N_HEADS
    const = lambda *shape: pl.BlockSpec(shape, lambda i: (0,) * len(shape))
    return pl.pallas_call(
        _mix_kernel,
        grid=(T // tm,),
        in_specs=[pl.BlockSpec((tm, D), lambda i: (i, 0)),
                  pl.BlockSpec((H, tm, LANES), lambda i: (0, i, 0)),
                  pl.BlockSpec((H, tm, LANES), lambda i: (0, i, 0)),
                  pl.BlockSpec((tm, 2 * D), lambda i: (i, 0)),
                  const(H, LANES, D), const(H, LANES, D), const(D, D), const(1, D), const(D, nqp)],
        out_specs=[pl.BlockSpec((tm, D), lambda i: (i, 0)),
                   pl.BlockSpec((tm, D), lambda i: (i, 0)),
                   pl.BlockSpec((tm, nqp), lambda i: (i, 0))],
        out_shape=[jax.ShapeDtypeStruct((T, D), F32),
                   jax.ShapeDtypeStruct((T, D), BF16),
                   jax.ShapeDtypeStruct((T, nqp), F32)],
        compiler_params=_cparams(("arbitrary",)),
        name="mix_out",
    )(x2d, oa, ob, gates, woa, wob, wmix, g2, wq)


_CAND_SLABS = ((0, 16, 0), (1, 8, 16), (2, 5, 24), (3, 4, 32), (4, 3, 40), (5, 2, 48), (6, 2, 56), (7, 2, 64))
_CAND_TAIL_ROW0 = 72
_CAND_ROWS = 80


def _topk_rank(x, k, top_ref, *, tie_break):
    R, n = x.shape
    row = lax.broadcasted_iota(jnp.int32, (R, n), 0)
    rank = jnp.full((R, n), k, jnp.int32)
    vals = []
    for r in range(k):
        m = jnp.max(x, axis=0, keepdims=True)
        hit = x == m
        if tie_break:
            hit = row == jnp.min(jnp.where(hit, row, R), axis=0, keepdims=True)
        rank = jnp.where(hit, r, rank)
        x = jnp.where(hit, -jnp.inf, x)
        vals.append(m)
        if top_ref is not None:
            top_ref[r:r + 1, :] = m
    count = jnp.sum((rank < k).astype(F32), axis=0, keepdims=True)
    return rank, vals, count


def _route_select(s1, s2, out_refs, top1_ref, top2_ref, cand_ref, *, tie_break):
    lr_ref, e1_ref, r2_ref, e2_ref = out_refs
    K = PEER_TOPK
    n = s1.shape[1]
    rank1, _, n1 = _topk_rank(s1, K, top1_ref, tie_break=tie_break)
    rank2, _, n2 = _topk_rank(s2, K, top2_ref, tie_break=tie_break)
    t1 = top1_ref[...]
    t2 = top2_ref[...]
    row8 = lax.broadcasted_iota(jnp.int32, (8, n), 0)
    for a, nvalid, off in _CAND_SLABS:
        rows = 16 if nvalid == 16 else 8
        slab = t1[a:a + 1, :] + t2[0:rows, :]
        if nvalid < rows:
            slab = jnp.where(row8 < nvalid, slab, -jnp.inf)
        cand_ref[off:off + rows, :] = slab
    cand_ref[_CAND_TAIL_ROW0:_CAND_ROWS, :] = t1[8:16, :] + t2[0:1, :]

    crank, cvals, nc = _topk_rank(cand_ref[...], K, None, tie_break=tie_break)
    sel = (crank < K).astype(F32)
    z = jnp.zeros((1, n), F32)
    for r in range(K):
        z = z + jnp.exp(cvals[r] - cvals[0])
    lr = jnp.zeros(s1.shape, F32)
    for a in range(K):
        if a < 8:
            _, nvalid, off = _CAND_SLABS[a]
            rows = 16 if nvalid == 16 else 8
            cnt = jnp.sum(sel[off:off + rows, :], axis=0, keepdims=True)
        else:
            cnt = sel[_CAND_TAIL_ROW0 + a - 8:_CAND_TAIL_ROW0 + a - 7, :]
        lr = jnp.where(rank1 == a, cnt, lr)
    lr_ref[0] = lr
    e1_ref[0] = jnp.exp(s1 - t1[0:1, :]) / z
    r2_ref[0] = rank2.astype(F32)
    e2_ref[0] = jnp.exp(s2 - t2[0:1, :])
    excess = jnp.maximum(jnp.maximum(jnp.abs(n1 - K), jnp.abs(n2 - K)), jnp.abs(nc - K))
    return jnp.max(excess) == 0.0


def _route_kernel(qp_ref, sk_ref, lr_ref, e1_ref, r2_ref, e2_ref, top1_ref, top2_ref, cand_ref):
    q = qp_ref[...]
    half = q.shape[1] // 2
    s1 = lax.dot_general(sk_ref[0, 0], q[:, :half], NT_DIMS, precision=lax.Precision.HIGHEST,
                         preferred_element_type=F32)
    s2 = lax.dot_general(sk_ref[0, 1], q[:, half:], NT_DIMS, precision=lax.Precision.HIGHEST,
                         preferred_element_type=F32)
    outs = (lr_ref, e1_ref, r2_ref, e2_ref)
    tie_free = _route_select(s1, s2, outs, top1_ref, top2_ref, cand_ref, tie_break=False)

    @pl.when(jnp.logical_not(tie_free))
    def _():
        _route_select(s1, s2, outs, top1_ref, top2_ref, cand_ref, tie_break=True)


def _route(qp, sub_keys, *, tt):
    T = qp.shape[0]
    PH, NK = PEER_HEADS, PEER_NKEYS
    qd = qp.shape[1] // PH
    out_f32 = jax.ShapeDtypeStruct((PH, NK, T), F32)
    ospec = pl.BlockSpec((1, NK, tt), lambda i, h: (h, 0, i))
    return pl.pallas_call(
        _route_kernel,
        grid=(T // tt, PH),
        in_specs=[pl.BlockSpec((tt, qd), lambda i, h: (i, h)),
                  pl.BlockSpec((1, 2, NK, qd // 2), lambda i, h: (h, 0, 0, 0))],
        out_specs=[ospec] * 4,
        out_shape=[out_f32] * 4,
        scratch_shapes=[pltpu.VMEM((PEER_TOPK, tt), F32), pltpu.VMEM((PEER_TOPK, tt), F32),
                        pltpu.VMEM((_CAND_ROWS, tt), F32)],
        compiler_params=_cparams(("arbitrary", "arbitrary")),
        name="peer_route",
    )(qp, sub_keys)


BF16_SUBLANES = 16
PEER_ROWBLOCKS_PER_DOT = 2


def _bcast_rows_bf16(row, n):
    tile = jnp.broadcast_to(row, (BF16_SUBLANES, row.shape[1])).astype(BF16)
    return jnp.concatenate([tile] * (n // BF16_SUBLANES), axis=0)


def _peer_kernel(x2_ref, h_ref, lr_ref, e1_ref, r2_ref, e2_ref, u_ref, vt_ref, gf_ref, o_ref, acc_ref,
                 r2p_ref, e2p_ref, *, n_i, final_norm):
    e = pl.program_id(1)
    NK = PEER_NKEYS

    @pl.when(e == 0)
    def _():
        acc_ref[...] = jnp.zeros_like(acc_ref)
        for hh in range(PEER_HEADS):
            r2p_ref[hh] = pltpu.bitcast(r2_ref[hh].astype(BF16), jnp.uint32)
            e2p_ref[hh] = pltpu.bitcast(e2_ref[hh].astype(BF16), jnp.uint32)

    tt = x2_ref.shape[0]
    zero = jnp.zeros((NK, LANES), BF16)
    cnts = [[lr_ref[hh, pl.ds(e * n_i + ii, 1), :] for hh in range(PEER_HEADS)] for ii in range(n_i)]
    e1s = [[e1_ref[hh, pl.ds(e * n_i + ii, 1), :] for hh in range(PEER_HEADS)] for ii in range(n_i)]
    x2 = x2_ref[...]
    w_rows = []
    for ii in range(n_i):
        if ii % PEER_ROWBLOCKS_PER_DOT == 0:
            r0 = ii * NK
            at = lax.dot_general(u_ref[r0:r0 + PEER_ROWBLOCKS_PER_DOT * NK, :], x2, NT_DIMS,
                                 preferred_element_type=F32)
        ar0 = (ii % PEER_ROWBLOCKS_PER_DOT) * NK
        w_cols = []
        for c0 in range(0, tt, LANES):
            cols = slice(c0, c0 + LANES)
            a = at[ar0:ar0 + NK, cols]
            ga = (0.5 * a * (1.0 + lax.erf(a * INV_SQRT2))).astype(BF16)
            coef = None
            for hh in range(PEER_HEADS):
                cnt = _bcast_rows_bf16(cnts[ii][hh][:, cols], NK)
                e1 = _bcast_rows_bf16(e1s[ii][hh][:, cols], NK)
                r2 = pltpu.bitcast(r2p_ref[hh, :, cols], BF16)
                e2 = pltpu.bitcast(e2p_ref[hh, :, cols], BF16)
                term = jnp.where(r2 < cnt, e2, zero) * e1
                coef = term if coef is None else coef + term
            w_cols.append(coef * ga)
        w_rows.append(jnp.concatenate(w_cols, axis=1))
    w = jnp.concatenate(w_rows, axis=0)
    acc_ref[...] += jnp.dot(vt_ref[...], w, preferred_element_type=F32)

    @pl.when(e == pl.num_programs(1) - 1)
    def _():
        y = h_ref[...] + acc_ref[...].T
        o_ref[...] = _rmsnorm(y, gf_ref[...]) if final_norm else y


def _peer(x2, h, lr, e1, r2, e2, u, vt, gf, *, tt, eb, final_norm):
    T, D = h.shape
    NE = u.shape[0]
    PH, NK = PEER_HEADS, PEER_NKEYS
    rspec = pl.BlockSpec((PH, NK, tt), lambda i, e: (0, 0, i))
    return pl.pallas_call(
        functools.partial(_peer_kernel, n_i=eb // NK, final_norm=final_norm),
        grid=(T // tt, NE // eb),
        in_specs=[pl.BlockSpec((tt, D), lambda i, e: (i, 0)),
                  pl.BlockSpec((tt, D), lambda i, e: (i, 0)),
                  rspec, rspec, rspec, rspec,
                  pl.BlockSpec((eb, D), lambda i, e: (e, 0)),
                  pl.BlockSpec((D, eb), lambda i, e: (0, e)),
                  pl.BlockSpec((1, D), lambda i, e: (0, 0))],
        out_specs=pl.BlockSpec((tt, D), lambda i, e: (i, 0)),
        out_shape=jax.ShapeDtypeStruct((T, D), F32),
        scratch_shapes=[pltpu.VMEM((D, tt), F32),
                        pltpu.VMEM((PH, NK // 2, tt), jnp.uint32), pltpu.VMEM((PH, NK // 2, tt), jnp.uint32)],
        compiler_params=_cparams(("arbitrary", "arbitrary")),
        name="peer_experts",
    )(x2, h, lr, e1, r2, e2, u, vt, gf)


def _pad_heads_cols(w):
    D, n = w.shape
    w = w.reshape(D, n // HEAD_DIM, HEAD_DIM)
    w = jnp.pad(w, ((0, 0), (0, 0), (0, LANES - HEAD_DIM)))
    return w.reshape(D, (n // HEAD_DIM) * LANES)


def _pad_heads_rows(w):
    n, D = w.shape
    w = w.reshape(n // HEAD_DIM, HEAD_DIM, D)
    return jnp.pad(w, ((0, 0), (0, LANES - HEAD_DIM), (0, 0)))


def kernel(x, norm1_g, w_in, w_out_moba, w_out_sb, w_mix_out, norm2_g, peer_w_q, peer_sub_keys, peer_u,
           peer_v, final_norm_g):
    B, S, D = x.shape
    T = B * S
    depth = w_in.shape[0]
    n_qkv = 6 * N_HEADS * HEAD_DIM
    tm = min(1024, T)
    slopes = jnp.exp2(-8.0 * jnp.arange(1, N_HEADS + 1, dtype=F32) / N_HEADS)
    tri = jnp.tril(jnp.ones((MOBA_BLOCK, MOBA_BLOCK), F32)).astype(BF16)

    h = x.reshape(T, D)
    out = h
    for l in range(depth):
        w_heads = _pad_heads_cols(w_in[l][:, :n_qkv]).astype(BF16)
        w_gates = w_in[l][:, n_qkv:].astype(BF16)
        g1 = norm1_g[l].reshape(1, D)
        qkv = _proj(h, g1, w_heads, per_head=True, tm=tm)
        gates = _proj(h, g1, w_gates, per_head=False, tm=tm)
        oa = _moba(qkv, slopes, B, S)
        ob = _sb(qkv, tri, B, S)
        h, x2, qp = _mix(h, oa, ob, gates,
                         _pad_heads_rows(w_out_moba[l]).astype(BF16),
                         _pad_heads_rows(w_out_sb[l]).astype(BF16),
                         w_mix_out[l].astype(BF16), norm2_g[l].reshape(1, D),
                         peer_w_q[l].astype(BF16), tm=min(256, T))
        lr, e1, r2, e2 = _route(qp, peer_sub_keys[l], tt=min(256, T))
        out = _peer(x2, h, lr, e1, r2, e2, peer_u[l].astype(BF16), peer_v[l].T.astype(BF16),
                    final_norm_g.reshape(1, D), tt=min(512, T), eb=1024, final_norm=(l == depth - 1))
        h = out
    return out.reshape(B, S, D)
```

```python
import functools

import jax
import jax.numpy as jnp
from jax import lax
from jax.experimental import pallas as pl
from jax.experimental.pallas import tpu as pltpu

F32 = jnp.float32
BF16 = jnp.bfloat16

HEAD_DIM = 64
N_HEADS = 8
LANES = 128
MOBA_BLOCK = 256
MOBA_TOPK = 3
PEER_HEADS = 8
PEER_NKEYS = 128
PEER_TOPK = 16
RMS_EPS = 1e-6
NEG_BIG = -1e30
INV_SQRT2 = 0.7071067811865476
VMEM_LIMIT = 56 * 1024 * 1024

LANE_BLK0 = 64
MAX_KV_BLOCKS = 32
LANE_QPOS = 96
LANE_KPOS = 99
MOBA_HEADS_PER_STEP = 2
MOBA_KV_CHUNK = 8
SB_HEADS_PER_STEP = 4

NT_DIMS = (((1,), (1,)), ((), ()))


def _cparams(sem):
    return pltpu.CompilerParams(dimension_semantics=sem, vmem_limit_bytes=VMEM_LIMIT)


def _rmsnorm(x, g):
    return x * lax.rsqrt(jnp.mean(x * x, axis=-1, keepdims=True) + RMS_EPS) * g


def _split_hi_lo(x):
    hi = x.astype(BF16)
    lo = (x - hi.astype(F32)).astype(BF16)
    return hi, lo


def _split3(x):
    a = x.astype(BF16).astype(F32)
    r = x - a
    b = r.astype(BF16).astype(F32)
    c = (r - b).astype(BF16).astype(F32)
    return a, b, c


def _proj_kernel(x_ref, g_ref, w_ref, o_ref, xn_ref, *, per_head):
    j = pl.program_id(1)

    @pl.when(j == 0)
    def _():
        xn_ref[...] = _rmsnorm(x_ref[...], g_ref[...]).astype(BF16)

    res = jnp.dot(xn_ref[...], w_ref[...], preferred_element_type=F32)
    if per_head:
        res = res * jnp.where((j == 0) | (j == 3), HEAD_DIM ** -0.5, 1.0)
        for h in range(N_HEADS):
            o_ref[h] = res[:, h * LANES:(h + 1) * LANES].astype(BF16)
    else:
        o_ref[...] = jax.nn.sigmoid(res).astype(BF16)


def _proj(x2d, g, w, *, per_head, tm):
    T, D = x2d.shape
    ncol = w.shape[1] // 1024
    if per_head:
        out_shape = jax.ShapeDtypeStruct((ncol * N_HEADS, T, LANES), BF16)
        out_spec = pl.BlockSpec((N_HEADS, tm, LANES), lambda i, j: (j, i, 0))
    else:
        out_shape = jax.ShapeDtypeStruct((T, w.shape[1]), BF16)
        out_spec = pl.BlockSpec((tm, 1024), lambda i, j: (i, j))
    return pl.pallas_call(
        functools.partial(_proj_kernel, per_head=per_head),
        grid=(T // tm, ncol),
        in_specs=[pl.BlockSpec((tm, D), lambda i, j: (i, 0)),
                  pl.BlockSpec((1, D), lambda i, j: (0, 0)),
                  pl.BlockSpec((D, 1024), lambda i, j: (0, j))],
        out_specs=out_spec,
        out_shape=out_shape,
        scratch_shapes=[pltpu.VMEM((tm, D), BF16)],
        compiler_params=_cparams(("arbitrary", "arbitrary")),
        name="proj_heads" if per_head else "proj_gates",
    )(x2d, g, w)


def _moba_kernel(slope_ref, q_ref, k_ref, v_ref, o_ref, kaug_ref, vaug_ref, kmean_ref, *, nb, groups):
    g = pl.program_id(0)
    i = pl.program_id(1)
    hg, tq, _ = q_ref.shape
    blk = MOBA_BLOCK
    head0 = (g % groups) * hg

    @pl.when(i == 0)
    def _():
        kmean_ref[...] = jnp.zeros_like(kmean_ref)
        row = lax.broadcasted_iota(jnp.int32, (blk, LANES), 0)
        lane = lax.broadcasted_iota(jnp.int32, (blk, LANES), 1)
        ones_q = jnp.where((lane >= LANE_QPOS) & (lane < LANE_QPOS + 3), 1.0, 0.0)
        for hh in range(hg):
            slope = slope_ref[head0 + hh]

            def fill(n, c, hh=hh, slope=slope):
                sl = pl.ds(pl.multiple_of(n * blk, blk), blk)
                kf = k_ref[hh, sl, :].astype(F32)
                kmean_ref[hh, pl.ds(n, 1), :] = jnp.mean(kf, axis=0, keepdims=True)
                pa, pb, pc = _split3(slope * (n * blk + row).astype(F32))
                feat = jnp.where(lane == LANE_BLK0 + n, 1.0, ones_q)
                feat = jnp.where(lane == LANE_KPOS, pa, feat)
                feat = jnp.where(lane == LANE_KPOS + 1, pb, feat)
                feat = jnp.where(lane == LANE_KPOS + 2, pc, feat)
                kaug_ref[hh, sl, :] = jnp.where(lane < HEAD_DIM, kf, feat).astype(BF16)
                vf = v_ref[hh, sl, :].astype(F32)
                vaug_ref[hh, sl, :] = jnp.where(lane == HEAD_DIM, 1.0, vf).astype(BF16)
                return c

            lax.fori_loop(0, nb, fill, 0)

    heads = range(hg)
    lane = lax.broadcasted_iota(jnp.int32, (tq, LANES), 1)
    is_blk_lane = (lane >= LANE_BLK0) & (lane < LANE_BLK0 + MAX_KV_BLOCKS)
    rowb = lax.broadcasted_iota(jnp.int32, (MAX_KV_BLOCKS, tq), 0)
    tpos = (i * blk + lax.broadcasted_iota(jnp.int32, (tq, LANES), 0)).astype(F32)
    own_sl = pl.ds(pl.multiple_of(i * blk, blk), blk)
    r = lax.broadcasted_iota(jnp.int32, (tq, blk), 0)
    c = lax.broadcasted_iota(jnp.int32, (tq, blk), 1)
    causal = c <= r

    qfs = [q_ref[hh].astype(F32) for hh in heads]
    gate = [lax.dot_general(kmean_ref[hh], qfs[hh], NT_DIMS, precision=lax.Precision.HIGHEST,
                            preferred_element_type=F32) for hh in heads]
    qowns, own_scores = [], []
    for hh in heads:
        pa, pb, pc = _split3(-slope_ref[head0 + hh] * tpos)
        feat = jnp.where(lane == LANE_QPOS, pa, jnp.where(lane == LANE_QPOS + 1, pb,
                         jnp.where(lane == LANE_QPOS + 2, pc,
                                   jnp.where((lane >= LANE_KPOS) & (lane < LANE_KPOS + 3), 1.0, 0.0))))
        qown = jnp.where(lane < HEAD_DIM, qfs[hh], feat).astype(BF16)
        qowns.append(qown)
        own_scores.append(lax.dot_general(qown, kaug_ref[hh, own_sl, :], NT_DIMS,
                                          preferred_element_type=F32))
    qaugs = []
    for hh in heads:
        bsm = jnp.where(rowb < i, gate[hh], -jnp.inf)
        sel = jnp.zeros(bsm.shape, jnp.bool_)
        for _ in range(MOBA_TOPK):
            m = jnp.max(bsm, axis=0, keepdims=True)
            p = jnp.min(jnp.where(bsm == m, rowb, MAX_KV_BLOCKS), axis=0, keepdims=True)
            hit = rowb == p
            sel = sel | (hit & (m > -jnp.inf))
            bsm = jnp.where(hit, -jnp.inf, bsm)
        bias_t = jnp.where(sel, 0.0, NEG_BIG)
        bias = jnp.concatenate([jnp.zeros((LANE_BLK0, tq), F32), bias_t,
                                jnp.zeros((LANES - LANE_QPOS, tq), F32)], axis=0).T
        qaugs.append(jnp.where(is_blk_lane, bias.astype(BF16), qowns[hh]))
    states = []
    for hh in heads:
        s0 = jnp.where(causal, own_scores[hh], NEG_BIG)
        m0 = jnp.max(s0, axis=1, keepdims=True)
        p0 = jnp.exp(s0 - m0)
        acc0 = jnp.dot(p0.astype(BF16), vaug_ref[hh, own_sl, :], preferred_element_type=F32)
        states.append((m0, acc0))

    span = MOBA_KV_CHUNK * blk

    def body(cidx, states):
        sl = pl.ds(pl.multiple_of(cidx * span, span), span)
        scores = [lax.dot_general(qaugs[hh], kaug_ref[hh, sl, :], NT_DIMS, preferred_element_type=F32)
                  for hh in range(hg)]
        new = []
        for hh in range(hg):
            m, acc = states[hh]
            s = scores[hh]
            mn = jnp.maximum(m, jnp.max(s, axis=1, keepdims=True))
            alpha = jnp.exp(m - mn)
            p = jnp.exp(s - mn)
            acc = alpha * acc + jnp.dot(p.astype(BF16), vaug_ref[hh, sl, :], preferred_element_type=F32)
            new.append((mn, acc))
        return tuple(new)

    n_chunks = (i + MOBA_KV_CHUNK - 1) // MOBA_KV_CHUNK
    states = lax.fori_loop(0, n_chunks, body, tuple(states))
    for hh in range(hg):
        acc = states[hh][1]
        denom = acc[:, HEAD_DIM:HEAD_DIM + 1]
        o_ref[hh] = jnp.where(lane < HEAD_DIM, acc / denom, 0.0).astype(BF16)


def _moba(qkv, slopes, B, S):
    T = B * S
    nq = S // MOBA_BLOCK
    nb = S // MOBA_BLOCK
    assert nb <= MAX_KV_BLOCKS and nb % MOBA_KV_CHUNK == 0
    H = N_HEADS
    hg = MOBA_HEADS_PER_STEP
    G = H // hg
    return pl.pallas_call(
        functools.partial(_moba_kernel, nb=nb, groups=G),
        grid=(B * G, nq),
        in_specs=[pl.BlockSpec(memory_space=pltpu.SMEM),
                  pl.BlockSpec((hg, MOBA_BLOCK, LANES), lambda g, i: (g % G, (g // G) * nq + i, 0)),
                  pl.BlockSpec((hg, S, LANES), lambda g, i: (G + g % G, g // G, 0)),
                  pl.BlockSpec((hg, S, LANES), lambda g, i: (2 * G + g % G, g // G, 0))],
        out_specs=pl.BlockSpec((hg, MOBA_BLOCK, LANES), lambda g, i: (g % G, (g // G) * nq + i, 0)),
        out_shape=jax.ShapeDtypeStruct((H, T, LANES), BF16),
        scratch_shapes=[pltpu.VMEM((hg, S, LANES), BF16), pltpu.VMEM((hg, S, LANES), BF16),
                        pltpu.VMEM((hg, MAX_KV_BLOCKS, LANES), F32)],
        compiler_params=_cparams(("arbitrary", "arbitrary")),
        name="moba_attn",
    )(slopes, qkv, qkv, qkv)


def _sb_kernel(q_ref, k_ref, v_ref, tri_ref, o_ref):
    i = pl.program_id(1)
    blk = MOBA_BLOCK
    hg, tq, _ = q_ref.shape
    tri = tri_ref[...]

    def tiles(j, states, diag):
        heads = range(hg)
        sl = pl.ds(pl.multiple_of(j * blk, blk), blk)
        zs = [lax.dot_general(q_ref[hh], k_ref[hh, sl, :], NT_DIMS, preferred_element_type=F32)
              for hh in heads]
        if diag:
            r = lax.broadcasted_iota(jnp.int32, (tq, blk), 0)
            c = lax.broadcasted_iota(jnp.int32, (tq, blk), 1)
            strict = c < r
        cums = []
        for hh in heads:
            z = zs[hh]
            lg = jnp.minimum(-z, 0.0) - jnp.log(1.0 + jnp.exp(-jnp.abs(z)))
            if diag:
                lg = jnp.where(strict, lg, 0.0)
            hi, lo = _split_hi_lo(lg)
            cums.append(jnp.dot(hi, tri, preferred_element_type=F32)
                        + jnp.dot(lo, tri, preferred_element_type=F32))
        new = []
        for hh in heads:
            carry, decay, acc = states[hh]
            w = jnp.exp(zs[hh] + cums[hh])
            if diag:
                w = jnp.where(strict, w, 0.0)
            pv = jnp.dot(w.astype(BF16), v_ref[hh, sl, :], preferred_element_type=F32)
            carry = carry + cums[hh][:, 0:1]
            new.append((carry, jnp.exp(carry), acc + decay * pv))
        return tuple(new)

    def any_alive(states):
        alive = jnp.max(states[0][1])
        for st in states[1:]:
            alive = jnp.maximum(alive, jnp.max(st[1]))
        return alive > 0.0

    init = (jnp.zeros((tq, 1), F32), jnp.ones((tq, 1), F32), jnp.zeros((tq, LANES), F32))
    states = tiles(i, (init,) * hg, True)

    def cond(c):
        t, alive, _ = c
        return (t < i) & alive

    def body(c):
        t, _, states = c
        states = tiles(i - 1 - t, states, False)
        return t + 1, any_alive(states), states

    _, _, states = lax.while_loop(cond, body, (jnp.int32(0), any_alive(states), states))
    for hh in range(hg):
        o_ref[hh] = states[hh][2].astype(BF16)


def _sb(qkv, tri, B, S):
    T = B * S
    nq = S // MOBA_BLOCK
    H = N_HEADS
    hg = SB_HEADS_PER_STEP
    G = H // hg
    return pl.pallas_call(
        _sb_kernel,
        grid=(B * G, nq),
        in_specs=[pl.BlockSpec((hg, MOBA_BLOCK, LANES), lambda g, i: (3 * G + g % G, (g // G) * nq + i, 0)),
                  pl.BlockSpec((hg, S, LANES), lambda g, i: (4 * G + g % G, g // G, 0)),
                  pl.BlockSpec((hg, S, LANES), lambda g, i: (5 * G + g % G, g // G, 0)),
                  pl.BlockSpec((MOBA_BLOCK, MOBA_BLOCK), lambda g, i: (0, 0))],
        out_specs=pl.BlockSpec((hg, MOBA_BLOCK, LANES), lambda g, i: (g % G, (g // G) * nq + i, 0)),
        out_shape=jax.ShapeDtypeStruct((H, T, LANES), BF16),
        compiler_params=_cparams(("arbitrary", "arbitrary")),
        name="sb_attn",
    )(qkv, qkv, qkv, tri)


def _mix_kernel(x_ref, oa_ref, ob_ref, gate_ref, woa_ref, wob_ref, wmix_ref, g2_ref, wq_ref,
                h_ref, x2_ref, qp_ref):
    D = x_ref.shape[1]
    ya = jnp.dot(oa_ref[0], woa_ref[0], preferred_element_type=F32)
    yb = jnp.dot(ob_ref[0], wob_ref[0], preferred_element_type=F32)
    for h in range(1, N_HEADS):
        ya = ya + jnp.dot(oa_ref[h], woa_ref[h], preferred_element_type=F32)
        yb = yb + jnp.dot(ob_ref[h], wob_ref[h], preferred_element_type=F32)
    mixed = gate_ref[:, :D].astype(F32) * ya + gate_ref[:, D:].astype(F32) * yb
    h = x_ref[...] + jnp.dot(mixed.astype(BF16), wmix_ref[...], preferred_element_type=F32)
    h_ref[...] = h
    x2 = _rmsnorm(h, g2_ref[...]).astype(BF16)
    x2_ref[...] = x2
    qp_ref[...] = jnp.dot(x2, wq_ref[...], preferred_element_type=F32)


def _mix(x2d, oa, ob, gates, woa, wob, wmix, g2, wq, *, tm):
    T, D = x2d.shape
    nqp = wq.shape[1]
    H = N_HEADS
    const = lambda *shape: pl.BlockSpec(shape, lambda i: (0,) * len(shape))
    return pl.pallas_call(
        _mix_kernel,
        grid=(T // tm,),
        in_specs=[pl.BlockSpec((tm, D), lambda i: (i, 0)),
                  pl.BlockSpec((H, tm, LANES), lambda i: (0, i, 0)),
                  pl.BlockSpec((H, tm, LANES), lambda i: (0, i, 0)),
                  pl.BlockSpec((tm, 2 * D), lambda i: (i, 0)),
                  const(H, LANES, D), const(H, LANES, D), const(D, D), const(1, D), const(D, nqp)],
        out_specs=[pl.BlockSpec((tm, D), lambda i: (i, 0)),
                   pl.BlockSpec((tm, D), lambda i: (i, 0)),
                   pl.BlockSpec((tm, nqp), lambda i: (i, 0))],
        out_shape=[jax.ShapeDtypeStruct((T, D), F32),
                   jax.ShapeDtypeStruct((T, D), BF16),
                   jax.ShapeDtypeStruct((T, nqp), F32)],
        compiler_params=_cparams(("arbitrary",)),
        name="mix_out",
    )(x2d, oa, ob, gates, woa, wob, wmix, g2, wq)


_CAND_SLABS = ((0, 16, 0), (1, 8, 16), (2, 5, 24), (3, 4, 32), (4, 3, 40), (5, 2, 48), (6, 2, 56), (7, 2, 64))
_CAND_TAIL_ROW0 = 72
_CAND_ROWS = 80


def _topk_rank(x, k, top_ref, *, tie_break):
    R, n = x.shape
    row = lax.broadcasted_iota(jnp.int32, (R, n), 0)
    rank = jnp.full((R, n), k, jnp.int32)
    vals = []
    for r in range(k):
        m = jnp.max(x, axis=0, keepdims=True)
        hit = x == m
        if tie_break:
            hit = row == jnp.min(jnp.where(hit, row, R), axis=0, keepdims=True)
        rank = jnp.where(hit, r, rank)
        x = jnp.where(hit, -jnp.inf, x)
        vals.append(m)
        if top_ref is not None:
            top_ref[r:r + 1, :] = m
    count = jnp.sum((rank < k).astype(F32), axis=0, keepdims=True)
    return rank, vals, count


def _route_select(s1, s2, out_refs, top1_ref, top2_ref, cand_ref, *, tie_break):
    lr_ref, e1_ref, r2_ref, e2_ref = out_refs
    K = PEER_TOPK
    n = s1.shape[1]
    rank1, _, n1 = _topk_rank(s1, K, top1_ref, tie_break=tie_break)
    rank2, _, n2 = _topk_rank(s2, K, top2_ref, tie_break=tie_break)
    t1 = top1_ref[...]
    t2 = top2_ref[...]
    row8 = lax.broadcasted_iota(jnp.int32, (8, n), 0)
    for a, nvalid, off in _CAND_SLABS:
        rows = 16 if nvalid == 16 else 8
        slab = t1[a:a + 1, :] + t2[0:rows, :]
        if nvalid < rows:
            slab = jnp.where(row8 < nvalid, slab, -jnp.inf)
        cand_ref[off:off + rows, :] = slab
    cand_ref[_CAND_TAIL_ROW0:_CAND_ROWS, :] = t1[8:16, :] + t2[0:1, :]

    crank, cvals, nc = _topk_rank(cand_ref[...], K, None, tie_break=tie_break)
    sel = (crank < K).astype(F32)
    z = jnp.zeros((1, n), F32)
    for r in range(K):
        z = z + jnp.exp(cvals[r] - cvals[0])
    lr = jnp.zeros(s1.shape, F32)
    for a in range(K):
        if a < 8:
            _, nvalid, off = _CAND_SLABS[a]
            rows = 16 if nvalid == 16 else 8
            cnt = jnp.sum(sel[off:off + rows, :], axis=0, keepdims=True)
        else:
            cnt = sel[_CAND_TAIL_ROW0 + a - 8:_CAND_TAIL_ROW0 + a - 7, :]
        lr = jnp.where(rank1 == a, cnt, lr)
    lr_ref[0] = lr
    e1_ref[0] = jnp.exp(s1 - t1[0:1, :]) / z
    r2_ref[0] = rank2.astype(F32)
    e2_ref[0] = jnp.exp(s2 - t2[0:1, :])
    excess = jnp.maximum(jnp.maximum(jnp.abs(n1 - K), jnp.abs(n2 - K)), jnp.abs(nc - K))
    return jnp.max(excess) == 0.0


def _route_kernel(qp_ref, sk_ref, lr_ref, e1_ref, r2_ref, e2_ref, top1_ref, top2_ref, cand_ref):
    q = qp_ref[...]
    half = q.shape[1] // 2
    s1 = lax.dot_general(sk_ref[0, 0], q[:, :half], NT_DIMS, precision=lax.Precision.HIGHEST,
                         preferred_element_type=F32)
    s2 = lax.dot_general(sk_ref[0, 1], q[:, half:], NT_DIMS, precision=lax.Precision.HIGHEST,
                         preferred_element_type=F32)
    outs = (lr_ref, e1_ref, r2_ref, e2_ref)
    tie_free = _route_select(s1, s2, outs, top1_ref, top2_ref, cand_ref, tie_break=False)

    @pl.when(jnp.logical_not(tie_free))
    def _():
        _route_select(s1, s2, outs, top1_ref, top2_ref, cand_ref, tie_break=True)


def _route(qp, sub_keys, *, tt):
    T = qp.shape[0]
    PH, NK = PEER_HEADS, PEER_NKEYS
    qd = qp.shape[1] // PH
    out_f32 = jax.ShapeDtypeStruct((PH, NK, T), F32)
    ospec = pl.BlockSpec((1, NK, tt), lambda i, h: (h, 0, i))
    return pl.pallas_call(
        _route_kernel,
        grid=(T // tt, PH),
        in_specs=[pl.BlockSpec((tt, qd), lambda i, h: (i, h)),
                  pl.BlockSpec((1, 2, NK, qd // 2), lambda i, h: (h, 0, 0, 0))],
        out_specs=[ospec] * 4,
        out_shape=[out_f32] * 4,
        scratch_shapes=[pltpu.VMEM((PEER_TOPK, tt), F32), pltpu.VMEM((PEER_TOPK, tt), F32),
                        pltpu.VMEM((_CAND_ROWS, tt), F32)],
        compiler_params=_cparams(("arbitrary", "arbitrary")),
        name="peer_route",
    )(qp, sub_keys)


BF16_SUBLANES = 16
PEER_ROWBLOCKS_PER_DOT = 2


def _bcast_rows_bf16(row, n):
    tile = jnp.broadcast_to(row, (BF16_SUBLANES, row.shape[1])).astype(BF16)
    return jnp.concatenate([tile] * (n // BF16_SUBLANES), axis=0)


def _peer_kernel(x2_ref, h_ref, lr_ref, e1_ref, r2_ref, e2_ref, u_ref, vt_ref, gf_ref, o_ref, acc_ref,
                 r2p_ref, e2p_ref, *, n_i, final_norm):
    e = pl.program_id(1)
    NK = PEER_NKEYS

    @pl.when(e == 0)
    def _():
        acc_ref[...] = jnp.zeros_like(acc_ref)
        for hh in range(PEER_HEADS):
            r2p_ref[hh] = pltpu.bitcast(r2_ref[hh].astype(BF16), jnp.uint32)
            e2p_ref[hh] = pltpu.bitcast(e2_ref[hh].astype(BF16), jnp.uint32)

    tt = x2_ref.shape[0]
    zero = jnp.zeros((NK, LANES), BF16)
    cnts = [[lr_ref[hh, pl.ds(e * n_i + ii, 1), :] for hh in range(PEER_HEADS)] for ii in range(n_i)]
    e1s = [[e1_ref[hh, pl.ds(e * n_i + ii, 1), :] for hh in range(PEER_HEADS)] for ii in range(n_i)]
    x2 = x2_ref[...]
    w_rows = []
    for ii in range(n_i):
        if ii % PEER_ROWBLOCKS_PER_DOT == 0:
            r0 = ii * NK
            at = lax.dot_general(u_ref[r0:r0 + PEER_ROWBLOCKS_PER_DOT * NK, :], x2, NT_DIMS,
                                 preferred_element_type=F32)
        ar0 = (ii % PEER_ROWBLOCKS_PER_DOT) * NK
        w_cols = []
        for c0 in range(0, tt, LANES):
            cols = slice(c0, c0 + LANES)
            a = at[ar0:ar0 + NK, cols]
            ga = (0.5 * a * (1.0 + lax.erf(a * INV_SQRT2))).astype(BF16)
            coef = None
            for hh in range(PEER_HEADS):
                cnt = _bcast_rows_bf16(cnts[ii][hh][:, cols], NK)
                e1 = _bcast_rows_bf16(e1s[ii][hh][:, cols], NK)
                r2 = pltpu.bitcast(r2p_ref[hh, :, cols], BF16)
                e2 = pltpu.bitcast(e2p_ref[hh, :, cols], BF16)
                term = jnp.where(r2 < cnt, e2, zero) * e1
                coef = term if coef is None else coef + term
            w_cols.append(coef * ga)
        w_rows.append(jnp.concatenate(w_cols, axis=1))
    w = jnp.concatenate(w_rows, axis=0)
    acc_ref[...] += jnp.dot(vt_ref[...], w, preferred_element_type=F32)

    @pl.when(e == pl.num_programs(1) - 1)
    def _():
        y = h_ref[...] + acc_ref[...].T
        o_ref[...] = _rmsnorm(y, gf_ref[...]) if final_norm else y


def _peer(x2, h, lr, e1, r2, e2, u, vt, gf, *, tt, eb, final_norm):
    T, D = h.shape
    NE = u.shape[0]
    PH, NK = PEER_HEADS, PEER_NKEYS
    rspec = pl.BlockSpec((PH, NK, tt), lambda i, e: (0, 0, i))
    return pl.pallas_call(
        functools.partial(_peer_kernel, n_i=eb // NK, final_norm=final_norm),
        grid=(T // tt, NE // eb),
        in_specs=[pl.BlockSpec((tt, D), lambda i, e: (i, 0)),
                  pl.BlockSpec((tt, D), lambda i, e: (i, 0)),
                  rspec, rspec, rspec, rspec,
                  pl.BlockSpec((eb, D), lambda i, e: (e, 0)),
                  pl.BlockSpec((D, eb), lambda i, e: (0, e)),
                  pl.BlockSpec((1, D), lambda i, e: (0, 0))],
        out_specs=pl.BlockSpec((tt, D), lambda i, e: (i, 0)),
        out_shape=jax.ShapeDtypeStruct((T, D), F32),
        scratch_shapes=[pltpu.VMEM((D, tt), F32),
                        pltpu.VMEM((PH, NK // 2, tt), jnp.uint32), pltpu.VMEM((PH, NK // 2, tt), jnp.uint32)],
        compiler_params=_cparams(("arbitrary", "arbitrary")),
        name="peer_experts",
    )(x2, h, lr, e1, r2, e2, u, vt, gf)


def _pad_heads_cols(w):
    D, n = w.shape
    w = w.reshape(D, n // HEAD_DIM, HEAD_DIM)
    w = jnp.pad(w, ((0, 0), (0, 0), (0, LANES - HEAD_DIM)))
    return w.reshape(D, (n // HEAD_DIM) * LANES)


def _pad_heads_rows(w):
    n, D = w.shape
    w = w.reshape(n // HEAD_DIM, HEAD_DIM, D)
    return jnp.pad(w, ((0, 0), (0, LANES - HEAD_DIM), (0, 0)))


def kernel(x, norm1_g, w_in, w_out_moba, w_out_sb, w_mix_out, norm2_g, peer_w_q, peer_sub_keys, peer_u,
           peer_v, final_norm_g):
    B, S, D = x.shape
    T = B * S
    depth = w_in.shape[0]
    n_qkv = 6 * N_HEADS * HEAD_DIM
    tm = min(1024, T)
    slopes = jnp.exp2(-8.0 * jnp.arange(1, N_HEADS + 1, dtype=F32) / N_HEADS)
    tri = jnp.tril(jnp.ones((MOBA_BLOCK, MOBA_BLOCK), F32)).astype(BF16)

    h = x.reshape(T, D)
    out = h
    for l in range(depth):
        w_heads = _pad_heads_cols(w_in[l][:, :n_qkv]).astype(BF16)
        w_gates = w_in[l][:, n_qkv:].astype(BF16)
        g1 = norm1_g[l].reshape(1, D)
        qkv = _proj(h, g1, w_heads, per_head=True, tm=tm)
        gates = _proj(h, g1, w_gates, per_head=False, tm=tm)
        oa = _moba(qkv, slopes, B, S)
        ob = _sb(qkv, tri, B, S)
        h, x2, qp = _mix(h, oa, ob, gates,
                         _pad_heads_rows(w_out_moba[l]).astype(BF16),
                         _pad_heads_rows(w_out_sb[l]).astype(BF16),
                         w_mix_out[l].astype(BF16), norm2_g[l].reshape(1, D),
                         peer_w_q[l].astype(BF16), tm=min(256, T))
        lr, e1, r2, e2 = _route(qp, peer_sub_keys[l], tt=min(256, T))
        out = _peer(x2, h, lr, e1, r2, e2, peer_u[l].astype(BF16), peer_v[l].T.astype(BF16),
                    final_norm_g.reshape(1, D), tt=min(512, T), eb=1024, final_norm=(l == depth - 1))
        h = out
    return out.reshape(B, S, D)
```

```python
import functools

import jax
import jax.numpy as jnp
from jax import lax
from jax.experimental import pallas as pl
from jax.experimental.pallas import tpu as pltpu

F32 = jnp.float32
BF16 = jnp.bfloat16

HEAD_DIM = 64
N_HEADS = 8
LANES = 128
MOBA_BLOCK = 256
MOBA_TOPK = 3
PEER_HEADS = 8
PEER_NKEYS = 128
PEER_TOPK = 16
RMS_EPS = 1e-6
NEG_BIG = -1e30
INV_SQRT2 = 0.7071067811865476
VMEM_LIMIT = 56 * 1024 * 1024

LANE_BLK0 = 64
MAX_KV_BLOCKS = 32
LANE_QPOS = 96
LANE_KPOS = 99
MOBA_HEADS_PER_STEP = 2
MOBA_KV_CHUNK = 8
SB_HEADS_PER_STEP = 4

NT_DIMS = (((1,), (1,)), ((), ()))


def _cparams(sem):
    return pltpu.CompilerParams(dimension_semantics=sem, vmem_limit_bytes=VMEM_LIMIT)


def _rmsnorm(x, g):
    return x * lax.rsqrt(jnp.mean(x * x, axis=-1, keepdims=True) + RMS_EPS) * g


def _split_hi_lo(x):
    hi = x.astype(BF16)
    lo = (x - hi.astype(F32)).astype(BF16)
    return hi, lo


def _split3(x):
    a = x.astype(BF16).astype(F32)
    r = x - a
    b = r.astype(BF16).astype(F32)
    c = (r - b).astype(BF16).astype(F32)
    return a, b, c


def _proj_kernel(x_ref, g_ref, w_ref, o_ref, xn_ref, *, per_head):
    j = pl.program_id(1)

    @pl.when(j == 0)
    def _():
        xn_ref[...] = _rmsnorm(x_ref[...], g_ref[...]).astype(BF16)

    res = jnp.dot(xn_ref[...], w_ref[...], preferred_element_type=F32)
    if per_head:
        res = res * jnp.where((j == 0) | (j == 3), HEAD_DIM ** -0.5, 1.0)
        for h in range(N_HEADS):
            o_ref[h] = res[:, h * LANES:(h + 1) * LANES].astype(BF16)
    else:
        o_ref[...] = jax.nn.sigmoid(res).astype(BF16)


def _proj(x2d, g, w, *, per_head, tm):
    T, D = x2d.shape
    ncol = w.shape[1] // 1024
    if per_head:
        out_shape = jax.ShapeDtypeStruct((ncol * N_HEADS, T, LANES), BF16)
        out_spec = pl.BlockSpec((N_HEADS, tm, LANES), lambda i, j: (j, i, 0))
    else:
        out_shape = jax.ShapeDtypeStruct((T, w.shape[1]), BF16)
        out_spec = pl.BlockSpec((tm, 1024), lambda i, j: (i, j))
    return pl.pallas_call(
        functools.partial(_proj_kernel, per_head=per_head),
        grid=(T // tm, ncol),
        in_specs=[pl.BlockSpec((tm, D), lambda i, j: (i, 0)),
                  pl.BlockSpec((1, D), lambda i, j: (0, 0)),
                  pl.BlockSpec((D, 1024), lambda i, j: (0, j))],
        out_specs=out_spec,
        out_shape=out_shape,
        scratch_shapes=[pltpu.VMEM((tm, D), BF16)],
        compiler_params=_cparams(("arbitrary", "arbitrary")),
        name="proj_heads" if per_head else "proj_gates",
    )(x2d, g, w)


def _moba_kernel(slope_ref, q_ref, k_ref, v_ref, o_ref, kaug_ref, vaug_ref, kmean_ref, *, nb, groups):
    g = pl.program_id(0)
    i = pl.program_id(1)
    hg, tq, _ = q_ref.shape
    blk = MOBA_BLOCK
    head0 = (g % groups) * hg

    @pl.when(i == 0)
    def _():
        kmean_ref[...] = jnp.zeros_like(kmean_ref)
        row = lax.broadcasted_iota(jnp.int32, (blk, LANES), 0)
        lane = lax.broadcasted_iota(jnp.int32, (blk, LANES), 1)
        ones_q = jnp.where((lane >= LANE_QPOS) & (lane < LANE_QPOS + 3), 1.0, 0.0)
        for hh in range(hg):
            slope = slope_ref[head0 + hh]

            def fill(n, c, hh=hh, slope=slope):
                sl = pl.ds(pl.multiple_of(n * blk, blk), blk)
                kf = k_ref[hh, sl, :].astype(F32)
                kmean_ref[hh, pl.ds(n, 1), :] = jnp.mean(kf, axis=0, keepdims=True)
                pa, pb, pc = _split3(slope * (n * blk + row).astype(F32))
                feat = jnp.where(lane == LANE_BLK0 + n, 1.0, ones_q)
                feat = jnp.where(lane == LANE_KPOS, pa, feat)
                feat = jnp.where(lane == LANE_KPOS + 1, pb, feat)
                feat = jnp.where(lane == LANE_KPOS + 2, pc, feat)
                kaug_ref[hh, sl, :] = jnp.where(lane < HEAD_DIM, kf, feat).astype(BF16)
                vf = v_ref[hh, sl, :].astype(F32)
                vaug_ref[hh, sl, :] = jnp.where(lane == HEAD_DIM, 1.0, vf).astype(BF16)
                return c

            lax.fori_loop(0, nb, fill, 0)

    heads = range(hg)
    lane = lax.broadcasted_iota(jnp.int32, (tq, LANES), 1)
    is_blk_lane = (lane >= LANE_BLK0) & (lane < LANE_BLK0 + MAX_KV_BLOCKS)
    rowb = lax.broadcasted_iota(jnp.int32, (MAX_KV_BLOCKS, tq), 0)
    tpos = (i * blk + lax.broadcasted_iota(jnp.int32, (tq, LANES), 0)).astype(F32)
    own_sl = pl.ds(pl.multiple_of(i * blk, blk), blk)
    r = lax.broadcasted_iota(jnp.int32, (tq, blk), 0)
    c = lax.broadcasted_iota(jnp.int32, (tq, blk), 1)
    causal = c <= r

    qfs = [q_ref[hh].astype(F32) for hh in heads]
    gate = [lax.dot_general(kmean_ref[hh], qfs[hh], NT_DIMS, precision=lax.Precision.HIGHEST,
                            preferred_element_type=F32) for hh in heads]
    qowns, own_scores = [], []
    for hh in heads:
        pa, pb, pc = _split3(-slope_ref[head0 + hh] * tpos)
        feat = jnp.where(lane == LANE_QPOS, pa, jnp.where(lane == LANE_QPOS + 1, pb,
                         jnp.where(lane == LANE_QPOS + 2, pc,
                                   jnp.where((lane >= LANE_KPOS) & (lane < LANE_KPOS + 3), 1.0, 0.0))))
        qown = jnp.where(lane < HEAD_DIM, qfs[hh], feat).astype(BF16)
        qowns.append(qown)
        own_scores.append(lax.dot_general(qown, kaug_ref[hh, own_sl, :], NT_DIMS,
                                          preferred_element_type=F32))
    qaugs = []
    for hh in heads:
        bsm = jnp.where(rowb < i, gate[hh], -jnp.inf)
        sel = jnp.zeros(bsm.shape, jnp.bool_)
        for _ in range(MOBA_TOPK):
            m = jnp.max(bsm, axis=0, keepdims=True)
            p = jnp.min(jnp.where(bsm == m, rowb, MAX_KV_BLOCKS), axis=0, keepdims=True)
            hit = rowb == p
            sel = sel | (hit & (m > -jnp.inf))
            bsm = jnp.where(hit, -jnp.inf, bsm)
        bias_t = jnp.where(sel, 0.0, NEG_BIG)
        bias = jnp.concatenate([jnp.zeros((LANE_BLK0, tq), F32), bias_t,
                                jnp.zeros((LANES - LANE_QPOS, tq), F32)], axis=0).T
        qaugs.append(jnp.where(is_blk_lane, bias.astype(BF16), qowns[hh]))
    states = []
    for hh in heads:
        s0 = jnp.where(causal, own_scores[hh], NEG_BIG)
        m0 = jnp.max(s0, axis=1, keepdims=True)
        p0 = jnp.exp(s0 - m0)
        acc0 = jnp.dot(p0.astype(BF16), vaug_ref[hh, own_sl, :], preferred_element_type=F32)
        states.append((m0, acc0))

    span = MOBA_KV_CHUNK * blk

    def body(cidx, states):
        sl = pl.ds(pl.multiple_of(cidx * span, span), span)
        scores = [lax.dot_general(qaugs[hh], kaug_ref[hh, sl, :], NT_DIMS, preferred_element_type=F32)
                  for hh in range(hg)]
        new = []
        for hh in range(hg):
            m, acc = states[hh]
            s = scores[hh]
            mn = jnp.maximum(m, jnp.max(s, axis=1, keepdims=True))
            alpha = jnp.exp(m - mn)
            p = jnp.exp(s - mn)
            acc = alpha * acc + jnp.dot(p.astype(BF16), vaug_ref[hh, sl, :], preferred_element_type=F32)
            new.append((mn, acc))
        return tuple(new)

    n_chunks = (i + MOBA_KV_CHUNK - 1) // MOBA_KV_CHUNK
    states = lax.fori_loop(0, n_chunks, body, tuple(states))
    for hh in range(hg):
        acc = states[hh][1]
        denom = acc[:, HEAD_DIM:HEAD_DIM + 1]
        o_ref[hh] = jnp.where(lane < HEAD_DIM, acc / denom, 0.0).astype(BF16)


def _moba(qkv, slopes, B, S):
    T = B * S
    nq = S // MOBA_BLOCK
    nb = S // MOBA_BLOCK
    assert nb <= MAX_KV_BLOCKS and nb % MOBA_KV_CHUNK == 0
    H = N_HEADS
    hg = MOBA_HEADS_PER_STEP
    G = H // hg
    return pl.pallas_call(
        functools.partial(_moba_kernel, nb=nb, groups=G),
        grid=(B * G, nq),
        in_specs=[pl.BlockSpec(memory_space=pltpu.SMEM),
                  pl.BlockSpec((hg, MOBA_BLOCK, LANES), lambda g, i: (g % G, (g // G) * nq + i, 0)),
                  pl.BlockSpec((hg, S, LANES), lambda g, i: (G + g % G, g // G, 0)),
                  pl.BlockSpec((hg, S, LANES), lambda g, i: (2 * G + g % G, g // G, 0))],
        out_specs=pl.BlockSpec((hg, MOBA_BLOCK, LANES), lambda g, i: (g % G, (g // G) * nq + i, 0)),
        out_shape=jax.ShapeDtypeStruct((H, T, LANES), BF16),
        scratch_shapes=[pltpu.VMEM((hg, S, LANES), BF16), pltpu.VMEM((hg, S, LANES), BF16),
                        pltpu.VMEM((hg, MAX_KV_BLOCKS, LANES), F32)],
        compiler_params=_cparams(("arbitrary", "arbitrary")),
        name="moba_attn",
    )(slopes, qkv, qkv, qkv)


def _sb_kernel(q_ref, k_ref, v_ref, tri_ref, o_ref):
    i = pl.program_id(1)
    blk = MOBA_BLOCK
    hg, tq, _ = q_ref.shape
    tri = tri_ref[...]

    def tiles(j, states, diag):
        heads = range(hg)
        sl = pl.ds(pl.multiple_of(j * blk, blk), blk)
        zs = [lax.dot_general(q_ref[hh], k_ref[hh, sl, :], NT_DIMS, preferred_element_type=F32)
              for hh in heads]
        if diag:
            r = lax.broadcasted_iota(jnp.int32, (tq, blk), 0)
            c = lax.broadcasted_iota(jnp.int32, (tq, blk), 1)
            strict = c < r
        cums = []
        for hh in heads:
            z = zs[hh]
            lg = jnp.minimum(-z, 0.0) - jnp.log(1.0 + jnp.exp(-jnp.abs(z)))
            if diag:
                lg = jnp.where(strict, lg, 0.0)
            hi, lo = _split_hi_lo(lg)
            cums.append(jnp.dot(hi, tri, preferred_element_type=F32)
                        + jnp.dot(lo, tri, preferred_element_type=F32))
        new = []
        for hh in heads:
            carry, decay, acc = states[hh]
            w = jnp.exp(zs[hh] + cums[hh])
            if diag:
                w = jnp.where(strict, w, 0.0)
            pv = jnp.dot(w.astype(BF16), v_ref[hh, sl, :], preferred_element_type=F32)
            carry = carry + cums[hh][:, 0:1]
            new.append((carry, jnp.exp(carry), acc + decay * pv))
        return tuple(new)

    def any_alive(states):
        alive = jnp.max(states[0][1])
        for st in states[1:]:
            alive = jnp.maximum(alive, jnp.max(st[1]))
        return alive > 0.0

    init = (jnp.zeros((tq, 1), F32), jnp.ones((tq, 1), F32), jnp.zeros((tq, LANES), F32))
    states = tiles(i, (init,) * hg, True)

    def cond(c):
        t, alive, _ = c
        return (t < i) & alive

    def body(c):
        t, _, states = c
        states = tiles(i - 1 - t, states, False)
        return t + 1, any_alive(states), states

    _, _, states = lax.while_loop(cond, body, (jnp.int32(0), any_alive(states), states))
    for hh in range(hg):
        o_ref[hh] = states[hh][2].astype(BF16)


def _sb(qkv, tri, B, S):
    T = B * S
    nq = S // MOBA_BLOCK
    H = N_HEADS
    hg = SB_HEADS_PER_STEP
    G = H // hg
    return pl.pallas_call(
        _sb_kernel,
        grid=(B * G, nq),
        in_specs=[pl.BlockSpec((hg, MOBA_BLOCK, LANES), lambda g, i: (3 * G + g % G, (g // G) * nq + i, 0)),
                  pl.BlockSpec((hg, S, LANES), lambda g, i: (4 * G + g % G, g // G, 0)),
                  pl.BlockSpec((hg, S, LANES), lambda g, i: (5 * G + g % G, g // G, 0)),
                  pl.BlockSpec((MOBA_BLOCK, MOBA_BLOCK), lambda g, i: (0, 0))],
        out_specs=pl.BlockSpec((hg, MOBA_BLOCK, LANES), lambda g, i: (g % G, (g // G) * nq + i, 0)),
        out_shape=jax.ShapeDtypeStruct((H, T, LANES), BF16),
        compiler_params=_cparams(("arbitrary", "arbitrary")),
        name="sb_attn",
    )(qkv, qkv, qkv, tri)


def _mix_kernel(x_ref, oa_ref, ob_ref, gate_ref, woa_ref, wob_ref, wmix_ref, g2_ref, wq_ref,
                h_ref, x2_ref, qp_ref):
    D = x_ref.shape[1]

    def out_proj(o_ref, w_ref):
        y = None
        for p in range(N_HEADS // 2):
            pair = jnp.concatenate([o_ref[2 * p], o_ref[2 * p + 1]], axis=1)
            part = jnp.dot(pair, w_ref[p], preferred_element_type=F32)
            y = part if y is None else y + part
        return y

    ya = out_proj(oa_ref, woa_ref)
    yb = out_proj(ob_ref, wob_ref)
    mixed = gate_ref[:, :D].astype(F32) * ya + gate_ref[:, D:].astype(F32) * yb
    h = x_ref[...] + jnp.dot(mixed.astype(BF16), wmix_ref[...], preferred_element_type=F32)
    h_ref[...] = h
    x2 = _rmsnorm(h, g2_ref[...]).astype(BF16)
    x2_ref[...] = x2
    qp_ref[...] = jnp.dot(x2, wq_ref[...], preferred_element_type=F32)


def _mix(x2d, oa, ob, gates, woa, wob, wmix, g2, wq, *, tm):
    T, D = x2d.shape
    nqp = wq.shape[1]
    H = N_HEADS
    const = lambda *shape: pl.BlockSpec(shape, lambda i: (0,) * len(shape))
    return pl.pallas_call(
        _mix_kernel,
        grid=(T // tm,),
        in_specs=[pl.BlockSpec((tm, D), lambda i: (i, 0)),
                  pl.BlockSpec((H, tm, LANES), lambda i: (0, i, 0)),
                  pl.BlockSpec((H, tm, LANES), lambda i: (0, i, 0)),
                  pl.BlockSpec((tm, 2 * D), lambda i: (i, 0)),
                  const(H // 2, 2 * LANES, D), const(H // 2, 2 * LANES, D), const(D, D), const(1, D),
                  const(D, nqp)],
        out_specs=[pl.BlockSpec((tm, D), lambda i: (i, 0)),
                   pl.BlockSpec((tm, D), lambda i: (i, 0)),
                   pl.BlockSpec((tm, nqp), lambda i: (i, 0))],
        out_shape=[jax.ShapeDtypeStruct((T, D), F32),
                   jax.ShapeDtypeStruct((T, D), BF16),
                   jax.ShapeDtypeStruct((T, nqp), F32)],
        compiler_params=_cparams(("arbitrary",)),
        name="mix_out",
    )(x2d, oa, ob, gates, woa, wob, wmix, g2, wq)


_CAND_SLABS = ((0, 16, 0), (1, 8, 16), (2, 5, 24), (3, 4, 32), (4, 3, 40), (5, 2, 48), (6, 2, 56), (7, 2, 64))
_CAND_TAIL_ROW0 = 72
_CAND_ROWS = 80


def _topk_rank(x, k, top_ref, *, tie_break):
    R, n = x.shape
    row = lax.broadcasted_iota(jnp.int32, (R, n), 0)
    rank = jnp.full((R, n), k, jnp.int32)
    vals = []
    for r in range(k):
        m = jnp.max(x, axis=0, keepdims=True)
        hit = x == m
        if tie_break:
            hit = row == jnp.min(jnp.where(hit, row, R), axis=0, keepdims=True)
        rank = jnp.where(hit, r, rank)
        x = jnp.where(hit, -jnp.inf, x)
        vals.append(m)
        if top_ref is not None:
            top_ref[r:r + 1, :] = m
    count = jnp.sum((rank < k).astype(F32), axis=0, keepdims=True)
    return rank, vals, count


def _route_select(s1, s2, out_refs, top1_ref, top2_ref, cand_ref, *, tie_break):
    lr_ref, e1_ref, r2_ref, e2_ref = out_refs
    K = PEER_TOPK
    n = s1.shape[1]
    rank1, _, n1 = _topk_rank(s1, K, top1_ref, tie_break=tie_break)
    rank2, _, n2 = _topk_rank(s2, K, top2_ref, tie_break=tie_break)
    t1 = top1_ref[...]
    t2 = top2_ref[...]
    row8 = lax.broadcasted_iota(jnp.int32, (8, n), 0)
    for a, nvalid, off in _CAND_SLABS:
        rows = 16 if nvalid == 16 else 8
        slab = t1[a:a + 1, :] + t2[0:rows, :]
        if nvalid < rows:
            slab = jnp.where(row8 < nvalid, slab, -jnp.inf)
        cand_ref[off:off + rows, :] = slab
    cand_ref[_CAND_TAIL_ROW0:_CAND_ROWS, :] = t1[8:16, :] + t2[0:1, :]

    crank, cvals, nc = _topk_rank(cand_ref[...], K, None, tie_break=tie_break)
    sel = (crank < K).astype(F32)
    z = jnp.zeros((1, n), F32)
    for r in range(K):
        z = z + jnp.exp(cvals[r] - cvals[0])
    lr = jnp.zeros(s1.shape, F32)
    for a in range(K):
        if a < 8:
            _, nvalid, off = _CAND_SLABS[a]
            rows = 16 if nvalid == 16 else 8
            cnt = jnp.sum(sel[off:off + rows, :], axis=0, keepdims=True)
        else:
            cnt = sel[_CAND_TAIL_ROW0 + a - 8:_CAND_TAIL_ROW0 + a - 7, :]
        lr = jnp.where(rank1 == a, cnt, lr)
    lr_ref[0] = lr
    e1_ref[0] = jnp.exp(s1 - t1[0:1, :]) / z
    r2_ref[0] = rank2.astype(F32)
    e2_ref[0] = jnp.exp(s2 - t2[0:1, :])
    excess = jnp.maximum(jnp.maximum(jnp.abs(n1 - K), jnp.abs(n2 - K)), jnp.abs(nc - K))
    return jnp.max(excess) == 0.0


def _route_kernel(qp_ref, sk_ref, lr_ref, e1_ref, r2_ref, e2_ref, top1_ref, top2_ref, cand_ref):
    q = qp_ref[...]
    half = q.shape[1] // 2
    s1 = lax.dot_general(sk_ref[0, 0], q[:, :half], NT_DIMS, precision=lax.Precision.HIGHEST,
                         preferred_element_type=F32)
    s2 = lax.dot_general(sk_ref[0, 1], q[:, half:], NT_DIMS, precision=lax.Precision.HIGHEST,
                         preferred_element_type=F32)
    outs = (lr_ref, e1_ref, r2_ref, e2_ref)
    tie_free = _route_select(s1, s2, outs, top1_ref, top2_ref, cand_ref, tie_break=False)

    @pl.when(jnp.logical_not(tie_free))
    def _():
        _route_select(s1, s2, outs, top1_ref, top2_ref, cand_ref, tie_break=True)


def _route(qp, sub_keys, *, tt):
    T = qp.shape[0]
    PH, NK = PEER_HEADS, PEER_NKEYS
    qd = qp.shape[1] // PH
    out_f32 = jax.ShapeDtypeStruct((PH, NK, T), F32)
    ospec = pl.BlockSpec((1, NK, tt), lambda i, h: (h, 0, i))
    return pl.pallas_call(
        _route_kernel,
        grid=(T // tt, PH),
        in_specs=[pl.BlockSpec((tt, qd), lambda i, h: (i, h)),
                  pl.BlockSpec((1, 2, NK, qd // 2), lambda i, h: (h, 0, 0, 0))],
        out_specs=[ospec] * 4,
        out_shape=[out_f32] * 4,
        scratch_shapes=[pltpu.VMEM((PEER_TOPK, tt), F32), pltpu.VMEM((PEER_TOPK, tt), F32),
                        pltpu.VMEM((_CAND_ROWS, tt), F32)],
        compiler_params=_cparams(("arbitrary", "arbitrary")),
        name="peer_route",
    )(qp, sub_keys)


BF16_SUBLANES = 16
PEER_ROWBLOCKS_PER_DOT = 2


def _bcast_rows_bf16(row, n):
    tile = jnp.broadcast_to(row, (BF16_SUBLANES, row.shape[1])).astype(BF16)
    return jnp.concatenate([tile] * (n // BF16_SUBLANES), axis=0)


def _peer_kernel(x2_ref, h_ref, lr_ref, e1_ref, r2_ref, e2_ref, u_ref, vt_ref, gf_ref, o_ref, acc_ref,
                 r2p_ref, e2p_ref, *, n_i, final_norm):
    e = pl.program_id(1)
    NK = PEER_NKEYS

    @pl.when(e == 0)
    def _():
        acc_ref[...] = jnp.zeros_like(acc_ref)
        for hh in range(PEER_HEADS):
            r2p_ref[hh] = pltpu.bitcast(r2_ref[hh].astype(BF16), jnp.uint32)
            e2p_ref[hh] = pltpu.bitcast(e2_ref[hh].astype(BF16), jnp.uint32)

    tt = x2_ref.shape[0]
    zero = jnp.zeros((NK, LANES), BF16)
    cnts = [[lr_ref[hh, pl.ds(e * n_i + ii, 1), :] for hh in range(PEER_HEADS)] for ii in range(n_i)]
    e1s = [[e1_ref[hh, pl.ds(e * n_i + ii, 1), :] for hh in range(PEER_HEADS)] for ii in range(n_i)]
    x2 = x2_ref[...]
    w_rows = []
    for ii in range(n_i):
        if ii % PEER_ROWBLOCKS_PER_DOT == 0:
            r0 = ii * NK
            at = lax.dot_general(u_ref[r0:r0 + PEER_ROWBLOCKS_PER_DOT * NK, :], x2, NT_DIMS,
                                 preferred_element_type=F32)
        ar0 = (ii % PEER_ROWBLOCKS_PER_DOT) * NK
        w_cols = []
        for c0 in range(0, tt, LANES):
            cols = slice(c0, c0 + LANES)
            a = at[ar0:ar0 + NK, cols]
            ga = (0.5 * a * (1.0 + lax.erf(a * INV_SQRT2))).astype(BF16)
            coef = None
            for hh in range(PEER_HEADS):
                cnt = _bcast_rows_bf16(cnts[ii][hh][:, cols], NK)
                e1 = _bcast_rows_bf16(e1s[ii][hh][:, cols], NK)
                r2 = pltpu.bitcast(r2p_ref[hh, :, cols], BF16)
                e2 = pltpu.bitcast(e2p_ref[hh, :, cols], BF16)
                term = jnp.where(r2 < cnt, e2, zero) * e1
                coef = term if coef is None else coef + term
            w_cols.append(coef * ga)
        w_rows.append(jnp.concatenate(w_cols, axis=1))
    w = jnp.concatenate(w_rows, axis=0)
    acc_ref[...] += jnp.dot(vt_ref[...], w, preferred_element_type=F32)

    @pl.when(e == pl.num_programs(1) - 1)
    def _():
        y = h_ref[...] + acc_ref[...].T
        o_ref[...] = _rmsnorm(y, gf_ref[...]) if final_norm else y


def _peer(x2, h, lr, e1, r2, e2, u, vt, gf, *, tt, eb, final_norm):
    T, D = h.shape
    NE = u.shape[0]
    PH, NK = PEER_HEADS, PEER_NKEYS
    rspec = pl.BlockSpec((PH, NK, tt), lambda i, e: (0, 0, i))
    return pl.pallas_call(
        functools.partial(_peer_kernel, n_i=eb // NK, final_norm=final_norm),
        grid=(T // tt, NE // eb),
        in_specs=[pl.BlockSpec((tt, D), lambda i, e: (i, 0)),
                  pl.BlockSpec((tt, D), lambda i, e: (i, 0)),
                  rspec, rspec, rspec, rspec,
                  pl.BlockSpec((eb, D), lambda i, e: (e, 0)),
                  pl.BlockSpec((D, eb), lambda i, e: (0, e)),
                  pl.BlockSpec((1, D), lambda i, e: (0, 0))],
        out_specs=pl.BlockSpec((tt, D), lambda i, e: (i, 0)),
        out_shape=jax.ShapeDtypeStruct((T, D), F32),
        scratch_shapes=[pltpu.VMEM((D, tt), F32),
                        pltpu.VMEM((PH, NK // 2, tt), jnp.uint32), pltpu.VMEM((PH, NK // 2, tt), jnp.uint32)],
        compiler_params=_cparams(("arbitrary", "arbitrary")),
        name="peer_experts",
    )(x2, h, lr, e1, r2, e2, u, vt, gf)


def _pad_heads_cols(w):
    D, n = w.shape
    w = w.reshape(D, n // HEAD_DIM, HEAD_DIM)
    w = jnp.pad(w, ((0, 0), (0, 0), (0, LANES - HEAD_DIM)))
    return w.reshape(D, (n // HEAD_DIM) * LANES)


def _pad_heads_rows(w):
    n, D = w.shape
    w = w.reshape(n // HEAD_DIM, HEAD_DIM, D)
    w = jnp.pad(w, ((0, 0), (0, LANES - HEAD_DIM), (0, 0)))
    return w.reshape(n // HEAD_DIM // 2, 2 * LANES, D)


def kernel(x, norm1_g, w_in, w_out_moba, w_out_sb, w_mix_out, norm2_g, peer_w_q, peer_sub_keys, peer_u,
           peer_v, final_norm_g):
    B, S, D = x.shape
    T = B * S
    depth = w_in.shape[0]
    n_qkv = 6 * N_HEADS * HEAD_DIM
    tm = min(1024, T)
    slopes = jnp.exp2(-8.0 * jnp.arange(1, N_HEADS + 1, dtype=F32) / N_HEADS)
    tri = jnp.tril(jnp.ones((MOBA_BLOCK, MOBA_BLOCK), F32)).astype(BF16)

    h = x.reshape(T, D)
    out = h
    for l in range(depth):
        w_heads = _pad_heads_cols(w_in[l][:, :n_qkv]).astype(BF16)
        w_gates = w_in[l][:, n_qkv:].astype(BF16)
        g1 = norm1_g[l].reshape(1, D)
        qkv = _proj(h, g1, w_heads, per_head=True, tm=tm)
        gates = _proj(h, g1, w_gates, per_head=False, tm=tm)
        oa = _moba(qkv, slopes, B, S)
        ob = _sb(qkv, tri, B, S)
        h, x2, qp = _mix(h, oa, ob, gates,
                         _pad_heads_rows(w_out_moba[l]).astype(BF16),
                         _pad_heads_rows(w_out_sb[l]).astype(BF16),
                         w_mix_out[l].astype(BF16), norm2_g[l].reshape(1, D),
                         peer_w_q[l].astype(BF16), tm=min(256, T))
        lr, e1, r2, e2 = _route(qp, peer_sub_keys[l], tt=min(512, T))
        out = _peer(x2, h, lr, e1, r2, e2, peer_u[l].astype(BF16), peer_v[l].astype(BF16).T,
                    final_norm_g.reshape(1, D), tt=min(512, T), eb=2048, final_norm=(l == depth - 1))
        h = out
    return out.reshape(B, S, D)
```

```python
import functools

import jax
import jax.numpy as jnp
from jax import lax
from jax.experimental import pallas as pl
from jax.experimental.pallas import tpu as pltpu

F32 = jnp.float32
BF16 = jnp.bfloat16

HEAD_DIM = 64
N_HEADS = 8
LANES = 128
MOBA_BLOCK = 256
MOBA_TOPK = 3
PEER_HEADS = 8
PEER_NKEYS = 128
PEER_TOPK = 16
RMS_EPS = 1e-6
NEG_BIG = -1e30
INV_SQRT2 = 0.7071067811865476
VMEM_LIMIT = 56 * 1024 * 1024

LANE_BLK0 = 64
MAX_KV_BLOCKS = 32
LANE_QPOS = 96
LANE_KPOS = 99
MOBA_HEADS_PER_STEP = 2
MOBA_KV_CHUNK = 8
SB_HEADS_PER_STEP = 4

NT_DIMS = (((1,), (1,)), ((), ()))


def _cparams(sem):
    return pltpu.CompilerParams(dimension_semantics=sem, vmem_limit_bytes=VMEM_LIMIT)


def _rmsnorm(x, g):
    return x * lax.rsqrt(jnp.mean(x * x, axis=-1, keepdims=True) + RMS_EPS) * g


def _split_hi_lo(x):
    hi = x.astype(BF16)
    lo = (x - hi.astype(F32)).astype(BF16)
    return hi, lo


def _split3(x):
    a = x.astype(BF16).astype(F32)
    r = x - a
    b = r.astype(BF16).astype(F32)
    c = (r - b).astype(BF16).astype(F32)
    return a, b, c


def _proj_kernel(x_ref, g_ref, w_ref, o_ref, xn_ref, *, per_head):
    j = pl.program_id(1)

    @pl.when(j == 0)
    def _():
        xn_ref[...] = _rmsnorm(x_ref[...], g_ref[...]).astype(BF16)

    res = jnp.dot(xn_ref[...], w_ref[...], preferred_element_type=F32)
    if per_head:
        res = res * jnp.where((j == 0) | (j == 3), HEAD_DIM ** -0.5, 1.0)
        for h in range(N_HEADS):
            o_ref[h] = res[:, h * LANES:(h + 1) * LANES].astype(BF16)
    else:
        o_ref[...] = jax.nn.sigmoid(res).astype(BF16)


def _proj(x2d, g, w, *, per_head, tm):
    T, D = x2d.shape
    ncol = w.shape[1] // 1024
    if per_head:
        out_shape = jax.ShapeDtypeStruct((ncol * N_HEADS, T, LANES), BF16)
        out_spec = pl.BlockSpec((N_HEADS, tm, LANES), lambda i, j: (j, i, 0))
    else:
        out_shape = jax.ShapeDtypeStruct((T, w.shape[1]), BF16)
        out_spec = pl.BlockSpec((tm, 1024), lambda i, j: (i, j))
    return pl.pallas_call(
        functools.partial(_proj_kernel, per_head=per_head),
        grid=(T // tm, ncol),
        in_specs=[pl.BlockSpec((tm, D), lambda i, j: (i, 0)),
                  pl.BlockSpec((1, D), lambda i, j: (0, 0)),
                  pl.BlockSpec((D, 1024), lambda i, j: (0, j))],
        out_specs=out_spec,
        out_shape=out_shape,
        scratch_shapes=[pltpu.VMEM((tm, D), BF16)],
        compiler_params=_cparams(("arbitrary", "arbitrary")),
        name="proj_heads" if per_head else "proj_gates",
    )(x2d, g, w)


def _moba_kernel(slope_ref, q_ref, k_ref, v_ref, o_ref, kaug_ref, vaug_ref, kmean_ref, *, nb, groups):
    g = pl.program_id(0)
    i = pl.program_id(1)
    hg, tq, _ = q_ref.shape
    blk = MOBA_BLOCK
    head0 = (g % groups) * hg

    @pl.when(i == 0)
    def _():
        kmean_ref[...] = jnp.zeros_like(kmean_ref)
        row = lax.broadcasted_iota(jnp.int32, (blk, LANES), 0)
        lane = lax.broadcasted_iota(jnp.int32, (blk, LANES), 1)
        ones_q = jnp.where((lane >= LANE_QPOS) & (lane < LANE_QPOS + 3), 1.0, 0.0)
        for hh in range(hg):
            slope = slope_ref[head0 + hh]

            def fill(n, c, hh=hh, slope=slope):
                sl = pl.ds(pl.multiple_of(n * blk, blk), blk)
                kf = k_ref[hh, sl, :].astype(F32)
                kmean_ref[hh, pl.ds(n, 1), :] = jnp.mean(kf, axis=0, keepdims=True)
                pa, pb, pc = _split3(slope * (n * blk + row).astype(F32))
                feat = jnp.where(lane == LANE_BLK0 + n, 1.0, ones_q)
                feat = jnp.where(lane == LANE_KPOS, pa, feat)
                feat = jnp.where(lane == LANE_KPOS + 1, pb, feat)
                feat = jnp.where(lane == LANE_KPOS + 2, pc, feat)
                kaug_ref[hh, sl, :] = jnp.where(lane < HEAD_DIM, kf, feat).astype(BF16)
                vf = v_ref[hh, sl, :].astype(F32)
                vaug_ref[hh, sl, :] = jnp.where(lane == HEAD_DIM, 1.0, vf).astype(BF16)
                return c

            lax.fori_loop(0, nb, fill, 0)

    heads = range(hg)
    lane = lax.broadcasted_iota(jnp.int32, (tq, LANES), 1)
    is_blk_lane = (lane >= LANE_BLK0) & (lane < LANE_BLK0 + MAX_KV_BLOCKS)
    rowb = lax.broadcasted_iota(jnp.int32, (MAX_KV_BLOCKS, tq), 0)
    tpos = (i * blk + lax.broadcasted_iota(jnp.int32, (tq, LANES), 0)).astype(F32)
    own_sl = pl.ds(pl.multiple_of(i * blk, blk), blk)
    r = lax.broadcasted_iota(jnp.int32, (tq, blk), 0)
    c = lax.broadcasted_iota(jnp.int32, (tq, blk), 1)
    causal = c <= r

    qfs = [q_ref[hh].astype(F32) for hh in heads]
    gate = [lax.dot_general(kmean_ref[hh], qfs[hh], NT_DIMS, precision=lax.Precision.HIGHEST,
                            preferred_element_type=F32) for hh in heads]
    qowns, own_scores = [], []
    for hh in heads:
        pa, pb, pc = _split3(-slope_ref[head0 + hh] * tpos)
        feat = jnp.where(lane == LANE_QPOS, pa, jnp.where(lane == LANE_QPOS + 1, pb,
                         jnp.where(lane == LANE_QPOS + 2, pc,
                                   jnp.where((lane >= LANE_KPOS) & (lane < LANE_KPOS + 3), 1.0, 0.0))))
        qown = jnp.where(lane < HEAD_DIM, qfs[hh], feat).astype(BF16)
        qowns.append(qown)
        own_scores.append(lax.dot_general(qown, kaug_ref[hh, own_sl, :], NT_DIMS,
                                          preferred_element_type=F32))
    qaugs = []
    for hh in heads:
        bsm = jnp.where(rowb < i, gate[hh], -jnp.inf)
        sel = jnp.zeros(bsm.shape, jnp.bool_)
        for _ in range(MOBA_TOPK):
            m = jnp.max(bsm, axis=0, keepdims=True)
            p = jnp.min(jnp.where(bsm == m, rowb, MAX_KV_BLOCKS), axis=0, keepdims=True)
            hit = rowb == p
            sel = sel | (hit & (m > -jnp.inf))
            bsm = jnp.where(hit, -jnp.inf, bsm)
        bias_t = jnp.where(sel, 0.0, NEG_BIG)
        bias = jnp.concatenate([jnp.zeros((LANE_BLK0, tq), F32), bias_t,
                                jnp.zeros((LANES - LANE_QPOS, tq), F32)], axis=0).T
        qaugs.append(jnp.where(is_blk_lane, bias.astype(BF16), qowns[hh]))
    states = []
    for hh in heads:
        s0 = jnp.where(causal, own_scores[hh], NEG_BIG)
        m0 = jnp.max(s0, axis=1, keepdims=True)
        p0 = jnp.exp(s0 - m0)
        acc0 = jnp.dot(p0.astype(BF16), vaug_ref[hh, own_sl, :], preferred_element_type=F32)
        states.append((m0, acc0))

    span = MOBA_KV_CHUNK * blk

    def body(cidx, states):
        sl = pl.ds(pl.multiple_of(cidx * span, span), span)
        scores = [lax.dot_general(qaugs[hh], kaug_ref[hh, sl, :], NT_DIMS, preferred_element_type=F32)
                  for hh in range(hg)]
        new = []
        for hh in range(hg):
            m, acc = states[hh]
            s = scores[hh]
            mn = jnp.maximum(m, jnp.max(s, axis=1, keepdims=True))
            alpha = jnp.exp(m - mn)
            p = jnp.exp(s - mn)
            acc = alpha * acc + jnp.dot(p.astype(BF16), vaug_ref[hh, sl, :], preferred_element_type=F32)
            new.append((mn, acc))
        return tuple(new)

    n_chunks = (i + MOBA_KV_CHUNK - 1) // MOBA_KV_CHUNK
    states = lax.fori_loop(0, n_chunks, body, tuple(states))
    for hh in range(hg):
        acc = states[hh][1]
        denom = acc[:, HEAD_DIM:HEAD_DIM + 1]
        o_ref[hh] = jnp.where(lane < HEAD_DIM, acc / denom, 0.0).astype(BF16)


def _moba(qkv, slopes, B, S):
    T = B * S
    nq = S // MOBA_BLOCK
    nb = S // MOBA_BLOCK
    assert nb <= MAX_KV_BLOCKS and nb % MOBA_KV_CHUNK == 0
    H = N_HEADS
    hg = MOBA_HEADS_PER_STEP
    G = H // hg
    return pl.pallas_call(
        functools.partial(_moba_kernel, nb=nb, groups=G),
        grid=(B * G, nq),
        in_specs=[pl.BlockSpec(memory_space=pltpu.SMEM),
                  pl.BlockSpec((hg, MOBA_BLOCK, LANES), lambda g, i: (g % G, (g // G) * nq + i, 0)),
                  pl.BlockSpec((hg, S, LANES), lambda g, i: (G + g % G, g // G, 0)),
                  pl.BlockSpec((hg, S, LANES), lambda g, i: (2 * G + g % G, g // G, 0))],
        out_specs=pl.BlockSpec((hg, MOBA_BLOCK, LANES), lambda g, i: (g % G, (g // G) * nq + i, 0)),
        out_shape=jax.ShapeDtypeStruct((H, T, LANES), BF16),
        scratch_shapes=[pltpu.VMEM((hg, S, LANES), BF16), pltpu.VMEM((hg, S, LANES), BF16),
                        pltpu.VMEM((hg, MAX_KV_BLOCKS, LANES), F32)],
        compiler_params=_cparams(("arbitrary", "arbitrary")),
        name="moba_attn",
    )(slopes, qkv, qkv, qkv)


def _sb_kernel(q_ref, k_ref, v_ref, tri_ref, o_ref):
    i = pl.program_id(1)
    blk = MOBA_BLOCK
    hg, tq, _ = q_ref.shape
    tri = tri_ref[...]

    def tiles(j, states, diag):
        heads = range(hg)
        sl = pl.ds(pl.multiple_of(j * blk, blk), blk)
        zs = [lax.dot_general(q_ref[hh], k_ref[hh, sl, :], NT_DIMS, preferred_element_type=F32)
              for hh in heads]
        if diag:
            r = lax.broadcasted_iota(jnp.int32, (tq, blk), 0)
            c = lax.broadcasted_iota(jnp.int32, (tq, blk), 1)
            strict = c < r
        cums = []
        for hh in heads:
            z = zs[hh]
            lg = jnp.minimum(-z, 0.0) - jnp.log(1.0 + jnp.exp(-jnp.abs(z)))
            if diag:
                lg = jnp.where(strict, lg, 0.0)
            hi, lo = _split_hi_lo(lg)
            cums.append(jnp.dot(hi, tri, preferred_element_type=F32)
                        + jnp.dot(lo, tri, preferred_element_type=F32))
        new = []
        for hh in heads:
            carry, decay, acc = states[hh]
            w = jnp.exp(zs[hh] + cums[hh])
            if diag:
                w = jnp.where(strict, w, 0.0)
            pv = jnp.dot(w.astype(BF16), v_ref[hh, sl, :], preferred_element_type=F32)
            carry = carry + cums[hh][:, 0:1]
            new.append((carry, jnp.exp(carry), acc + decay * pv))
        return tuple(new)

    def any_alive(states):
        alive = jnp.max(states[0][1])
        for st in states[1:]:
            alive = jnp.maximum(alive, jnp.max(st[1]))
        return alive > 0.0

    init = (jnp.zeros((tq, 1), F32), jnp.ones((tq, 1), F32), jnp.zeros((tq, LANES), F32))
    states = tiles(i, (init,) * hg, True)

    def cond(c):
        t, alive, _ = c
        return (t < i) & alive

    def body(c):
        t, _, states = c
        states = tiles(i - 1 - t, states, False)
        return t + 1, any_alive(states), states

    _, _, states = lax.while_loop(cond, body, (jnp.int32(0), any_alive(states), states))
    for hh in range(hg):
        o_ref[hh] = states[hh][2].astype(BF16)


def _sb(qkv, tri, B, S):
    T = B * S
    nq = S // MOBA_BLOCK
    H = N_HEADS
    hg = SB_HEADS_PER_STEP
    G = H // hg
    return pl.pallas_call(
        _sb_kernel,
        grid=(B * G, nq),
        in_specs=[pl.BlockSpec((hg, MOBA_BLOCK, LANES), lambda g, i: (3 * G + g % G, (g // G) * nq + i, 0)),
                  pl.BlockSpec((hg, S, LANES), lambda g, i: (4 * G + g % G, g // G, 0)),
                  pl.BlockSpec((hg, S, LANES), lambda g, i: (5 * G + g % G, g // G, 0)),
                  pl.BlockSpec((MOBA_BLOCK, MOBA_BLOCK), lambda g, i: (0, 0))],
        out_specs=pl.BlockSpec((hg, MOBA_BLOCK, LANES), lambda g, i: (g % G, (g // G) * nq + i, 0)),
        out_shape=jax.ShapeDtypeStruct((H, T, LANES), BF16),
        compiler_params=_cparams(("arbitrary", "arbitrary")),
        name="sb_attn",
    )(qkv, qkv, qkv, tri)


def _mix_kernel(x_ref, oa_ref, ob_ref, gate_ref, woa_ref, wob_ref, wmix_ref, g2_ref, wq_ref,
                h_ref, x2_ref, qp_ref):
    D = x_ref.shape[1]

    def out_proj(o_ref, w_ref):
        y = None
        for p in range(N_HEADS // 2):
            pair = jnp.concatenate([o_ref[2 * p], o_ref[2 * p + 1]], axis=1)
            part = jnp.dot(pair, w_ref[p], preferred_element_type=F32)
            y = part if y is None else y + part
        return y

    ya = out_proj(oa_ref, woa_ref)
    yb = out_proj(ob_ref, wob_ref)
    mixed = gate_ref[:, :D].astype(F32) * ya + gate_ref[:, D:].astype(F32) * yb
    h = x_ref[...] + jnp.dot(mixed.astype(BF16), wmix_ref[...], preferred_element_type=F32)
    h_ref[...] = h
    x2 = _rmsnorm(h, g2_ref[...]).astype(BF16)
    x2_ref[...] = x2
    qp_ref[...] = jnp.dot(x2, wq_ref[...], preferred_element_type=F32)


def _mix(x2d, oa, ob, gates, woa, wob, wmix, g2, wq, *, tm):
    T, D = x2d.shape
    nqp = wq.shape[1]
    H = N_HEADS
    const = lambda *shape: pl.BlockSpec(shape, lambda i: (0,) * len(shape))
    return pl.pallas_call(
        _mix_kernel,
        grid=(T // tm,),
        in_specs=[pl.BlockSpec((tm, D), lambda i: (i, 0)),
                  pl.BlockSpec((H, tm, LANES), lambda i: (0, i, 0)),
                  pl.BlockSpec((H, tm, LANES), lambda i: (0, i, 0)),
                  pl.BlockSpec((tm, 2 * D), lambda i: (i, 0)),
                  const(H // 2, 2 * LANES, D), const(H // 2, 2 * LANES, D), const(D, D), const(1, D),
                  const(D, nqp)],
        out_specs=[pl.BlockSpec((tm, D), lambda i: (i, 0)),
                   pl.BlockSpec((tm, D), lambda i: (i, 0)),
                   pl.BlockSpec((tm, nqp), lambda i: (i, 0))],
        out_shape=[jax.ShapeDtypeStruct((T, D), F32),
                   jax.ShapeDtypeStruct((T, D), BF16),
                   jax.ShapeDtypeStruct((T, nqp), F32)],
        compiler_params=_cparams(("arbitrary",)),
        name="mix_out",
    )(x2d, oa, ob, gates, woa, wob, wmix, g2, wq)


_CAND_SLABS = ((0, 16, 0), (1, 8, 16), (2, 5, 24), (3, 4, 32), (4, 3, 40), (5, 2, 48), (6, 2, 56), (7, 2, 64))
_CAND_TAIL_ROW0 = 72
_CAND_ROWS = 80


def _topk_rank(x, k, top_ref, *, tie_break):
    R, n = x.shape
    row = lax.broadcasted_iota(jnp.int32, (R, n), 0)
    rank = jnp.full((R, n), k, jnp.int32)
    vals = []
    for r in range(k):
        m = jnp.max(x, axis=0, keepdims=True)
        hit = x == m
        if tie_break:
            hit = row == jnp.min(jnp.where(hit, row, R), axis=0, keepdims=True)
        rank = jnp.where(hit, r, rank)
        x = jnp.where(hit, -jnp.inf, x)
        vals.append(m)
        if top_ref is not None:
            top_ref[r:r + 1, :] = m
    count = jnp.sum((rank < k).astype(F32), axis=0, keepdims=True)
    return rank, vals, count


def _route_select(s1, s2, write, top1_ref, top2_ref, cand_ref, *, tie_break):
    K = PEER_TOPK
    n = s1.shape[1]
    rank1, _, n1 = _topk_rank(s1, K, top1_ref, tie_break=tie_break)
    rank2, _, n2 = _topk_rank(s2, K, top2_ref, tie_break=tie_break)
    t1 = top1_ref[...]
    t2 = top2_ref[...]
    row8 = lax.broadcasted_iota(jnp.int32, (8, n), 0)
    for a, nvalid, off in _CAND_SLABS:
        rows = 16 if nvalid == 16 else 8
        slab = t1[a:a + 1, :] + t2[0:rows, :]
        if nvalid < rows:
            slab = jnp.where(row8 < nvalid, slab, -jnp.inf)
        cand_ref[off:off + rows, :] = slab
    cand_ref[_CAND_TAIL_ROW0:_CAND_ROWS, :] = t1[8:16, :] + t2[0:1, :]

    crank, cvals, nc = _topk_rank(cand_ref[...], K, None, tie_break=tie_break)
    sel = (crank < K).astype(F32)
    z = jnp.zeros((1, n), F32)
    for r in range(K):
        z = z + jnp.exp(cvals[r] - cvals[0])
    lr = jnp.zeros(s1.shape, F32)
    for a in range(K):
        if a < 8:
            _, nvalid, off = _CAND_SLABS[a]
            rows = 16 if nvalid == 16 else 8
            cnt = jnp.sum(sel[off:off + rows, :], axis=0, keepdims=True)
        else:
            cnt = sel[_CAND_TAIL_ROW0 + a - 8:_CAND_TAIL_ROW0 + a - 7, :]
        lr = jnp.where(rank1 == a, cnt, lr)
    write(lr, jnp.exp(s1 - t1[0:1, :]) / z, rank2.astype(F32), jnp.exp(s2 - t2[0:1, :]))
    excess = jnp.maximum(jnp.maximum(jnp.abs(n1 - K), jnp.abs(n2 - K)), jnp.abs(nc - K))
    return jnp.max(excess) == 0.0


BF16_SUBLANES = 16
PEER_ROWBLOCKS_PER_DOT = 2


def _bcast_rows_bf16(row, n):
    tile = jnp.broadcast_to(row, (BF16_SUBLANES, row.shape[1])).astype(BF16)
    return jnp.concatenate([tile] * (n // BF16_SUBLANES), axis=0)


def _peer_kernel(qp_ref, sk_ref, x2_ref, h_ref, u_ref, vt_ref, gf_ref, o_ref,
                 acc_ref, cnt_s, e1_s, r2p_s, e2p_s, top1_ref, top2_ref, cand_ref,
                 *, n_i, n_tiles, final_norm):
    i = pl.program_id(0)
    e = pl.program_id(1)
    last = pl.num_programs(1) - 1
    wslot = i % 2
    rslot = 1 - wslot
    NK = PEER_NKEYS
    tt = x2_ref.shape[0]

    def scores():
        q = qp_ref[...]
        half = q.shape[1] // 2
        s1 = lax.dot_general(sk_ref[0, 0], q[:, :half], NT_DIMS, precision=lax.Precision.HIGHEST,
                             preferred_element_type=F32)
        s2 = lax.dot_general(sk_ref[0, 1], q[:, half:], NT_DIMS, precision=lax.Precision.HIGHEST,
                             preferred_element_type=F32)
        return s1, s2

    def write(cnt, e1, rank2, e2):
        cnt_s[wslot, e] = cnt
        e1_s[wslot, e] = e1
        r2p_s[wslot, e] = pltpu.bitcast(rank2.astype(BF16), jnp.uint32)
        e2p_s[wslot, e] = pltpu.bitcast(e2.astype(BF16), jnp.uint32)

    def route(s1, s2, tie_break):
        return _route_select(s1, s2, write, top1_ref, top2_ref, cand_ref, tie_break=tie_break)

    def route_exact_if_tied(s1, s2, tie_free):
        @pl.when(jnp.logical_not(tie_free))
        def _():
            route(s1, s2, True)

    def experts():
        zero = jnp.zeros((NK, LANES), BF16)
        cnts = [[cnt_s[rslot, hh, pl.ds(e * n_i + ii, 1), :] for hh in range(PEER_HEADS)]
                for ii in range(n_i)]
        e1s = [[e1_s[rslot, hh, pl.ds(e * n_i + ii, 1), :] for hh in range(PEER_HEADS)]
               for ii in range(n_i)]
        x2 = x2_ref[...]
        w_rows = []
        for ii in range(n_i):
            if ii % PEER_ROWBLOCKS_PER_DOT == 0:
                r0 = ii * NK
                at = lax.dot_general(u_ref[r0:r0 + PEER_ROWBLOCKS_PER_DOT * NK, :], x2, NT_DIMS,
                                     preferred_element_type=F32)
            ar0 = (ii % PEER_ROWBLOCKS_PER_DOT) * NK
            w_cols = []
            for c0 in range(0, tt, LANES):
                cols = slice(c0, c0 + LANES)
                a = at[ar0:ar0 + NK, cols]
                ga = (0.5 * a * (1.0 + lax.erf(a * INV_SQRT2))).astype(BF16)
                coef = None
                for hh in range(PEER_HEADS):
                    cnt = _bcast_rows_bf16(cnts[ii][hh][:, cols], NK)
                    e1 = _bcast_rows_bf16(e1s[ii][hh][:, cols], NK)
                    r2 = pltpu.bitcast(r2p_s[rslot, hh, :, cols], BF16)
                    e2 = pltpu.bitcast(e2p_s[rslot, hh, :, cols], BF16)
                    term = jnp.where(r2 < cnt, e2, zero) * e1
                    coef = term if coef is None else coef + term
                w_cols.append(coef * ga)
            w_rows.append(jnp.concatenate(w_cols, axis=1))
        w = jnp.concatenate(w_rows, axis=0)
        acc_ref[...] += jnp.dot(vt_ref[...], w, preferred_element_type=F32)

    def finish_tile():
        @pl.when(e == last)
        def _():
            y = h_ref[...] + acc_ref[...].T
            o_ref[...] = _rmsnorm(y, gf_ref[...]) if final_norm else y
            acc_ref[...] = jnp.zeros_like(acc_ref)

    @pl.when(i == 0)
    def _():
        @pl.when(e == 0)
        def _():
            acc_ref[...] = jnp.zeros_like(acc_ref)

        s1, s2 = scores()
        route_exact_if_tied(s1, s2, route(s1, s2, False))

    @pl.when((i > 0) & (i < n_tiles))
    def _():
        s1, s2 = scores()
        tie_free = route(s1, s2, False)
        experts()
        route_exact_if_tied(s1, s2, tie_free)
        finish_tile()

    @pl.when(i == n_tiles)
    def _():
        experts()
        finish_tile()


def _peer(qp, sub_keys, x2, h, u, vt, gf, *, tt, eb, final_norm):
    T, D = h.shape
    NE = u.shape[0]
    PH, NK = PEER_HEADS, PEER_NKEYS
    n_tiles = T // tt
    assert NE // eb == PH
    qd = qp.shape[1] // PH
    prev = lambda i: jnp.maximum(i - 1, 0)
    blk = lambda i, e: jnp.where(i == 0, 0, e)
    return pl.pallas_call(
        functools.partial(_peer_kernel, n_i=eb // NK, n_tiles=n_tiles, final_norm=final_norm),
        grid=(n_tiles + 1, PH),
        in_specs=[pl.BlockSpec((tt, qd), lambda i, e: (jnp.minimum(i, n_tiles - 1), e)),
                  pl.BlockSpec((1, 2, NK, qd // 2), lambda i, e: (e, 0, 0, 0)),
                  pl.BlockSpec((tt, D), lambda i, e: (prev(i), 0)),
                  pl.BlockSpec((tt, D), lambda i, e: (prev(i), 0)),
                  pl.BlockSpec((eb, D), lambda i, e: (blk(i, e), 0)),
                  pl.BlockSpec((D, eb), lambda i, e: (0, blk(i, e))),
                  pl.BlockSpec((1, D), lambda i, e: (0, 0))],
        out_specs=pl.BlockSpec((tt, D), lambda i, e: (prev(i), 0)),
        out_shape=jax.ShapeDtypeStruct((T, D), F32),
        scratch_shapes=[pltpu.VMEM((D, tt), F32),
                        pltpu.VMEM((2, PH, NK, tt), F32), pltpu.VMEM((2, PH, NK, tt), F32),
                        pltpu.VMEM((2, PH, NK // 2, tt), jnp.uint32),
                        pltpu.VMEM((2, PH, NK // 2, tt), jnp.uint32),
                        pltpu.VMEM((PEER_TOPK, tt), F32), pltpu.VMEM((PEER_TOPK, tt), F32),
                        pltpu.VMEM((_CAND_ROWS, tt), F32)],
        compiler_params=_cparams(("arbitrary", "arbitrary")),
        name="peer_route_experts",
    )(qp, sub_keys, x2, h, u, vt, gf)


def _pad_heads_cols(w):
    D, n = w.shape
    w = w.reshape(D, n // HEAD_DIM, HEAD_DIM)
    w = jnp.pad(w, ((0, 0), (0, 0), (0, LANES - HEAD_DIM)))
    return w.reshape(D, (n // HEAD_DIM) * LANES)


def _pad_heads_rows(w):
    n, D = w.shape
    w = w.reshape(n // HEAD_DIM, HEAD_DIM, D)
    w = jnp.pad(w, ((0, 0), (0, LANES - HEAD_DIM), (0, 0)))
    return w.reshape(n // HEAD_DIM // 2, 2 * LANES, D)


def kernel(x, norm1_g, w_in, w_out_moba, w_out_sb, w_mix_out, norm2_g, peer_w_q, peer_sub_keys, peer_u,
           peer_v, final_norm_g):
    B, S, D = x.shape
    T = B * S
    depth = w_in.shape[0]
    n_qkv = 6 * N_HEADS * HEAD_DIM
    tm = min(1024, T)
    slopes = jnp.exp2(-8.0 * jnp.arange(1, N_HEADS + 1, dtype=F32) / N_HEADS)
    tri = jnp.tril(jnp.ones((MOBA_BLOCK, MOBA_BLOCK), F32)).astype(BF16)

    h = x.reshape(T, D)
    out = h
    for l in range(depth):
        w_heads = _pad_heads_cols(w_in[l][:, :n_qkv]).astype(BF16)
        w_gates = w_in[l][:, n_qkv:].astype(BF16)
        g1 = norm1_g[l].reshape(1, D)
        qkv = _proj(h, g1, w_heads, per_head=True, tm=tm)
        gates = _proj(h, g1, w_gates, per_head=False, tm=tm)
        oa = _moba(qkv, slopes, B, S)
        ob = _sb(qkv, tri, B, S)
        h, x2, qp = _mix(h, oa, ob, gates,
                         _pad_heads_rows(w_out_moba[l]).astype(BF16),
                         _pad_heads_rows(w_out_sb[l]).astype(BF16),
                         w_mix_out[l].astype(BF16), norm2_g[l].reshape(1, D),
                         peer_w_q[l].astype(BF16), tm=min(256, T))
        out = _peer(qp, peer_sub_keys[l], x2, h, peer_u[l].astype(BF16), peer_v[l].astype(BF16).T,
                    final_norm_g.reshape(1, D), tt=min(512, T), eb=2048, final_norm=(l == depth - 1))
        h = out
    return out.reshape(B, S, D)
```

```python
import functools

import jax
import jax.numpy as jnp
from jax import lax
from jax.experimental import pallas as pl
from jax.experimental.pallas import tpu as pltpu

F32 = jnp.float32
BF16 = jnp.bfloat16

HEAD_DIM = 64
N_HEADS = 8
LANES = 128
MOBA_BLOCK = 256
MOBA_TOPK = 3
PEER_HEADS = 8
PEER_NKEYS = 128
PEER_TOPK = 16
RMS_EPS = 1e-6
NEG_BIG = -1e30
INV_SQRT2 = 0.7071067811865476
VMEM_LIMIT = 56 * 1024 * 1024

LANE_BLK0 = 64
MAX_KV_BLOCKS = 32
LANE_QPOS = 96
LANE_KPOS = 99
MOBA_HEADS_PER_STEP = 2
MOBA_KV_CHUNK = 8
SB_HEADS_PER_STEP = 4

NT_DIMS = (((1,), (1,)), ((), ()))


def _cparams(sem):
    return pltpu.CompilerParams(dimension_semantics=sem, vmem_limit_bytes=VMEM_LIMIT)


def _rmsnorm(x, g):
    return x * lax.rsqrt(jnp.mean(x * x, axis=-1, keepdims=True) + RMS_EPS) * g


def _split_hi_lo(x):
    hi = x.astype(BF16)
    lo = (x - hi.astype(F32)).astype(BF16)
    return hi, lo


def _split3(x):
    a = x.astype(BF16).astype(F32)
    r = x - a
    b = r.astype(BF16).astype(F32)
    c = (r - b).astype(BF16).astype(F32)
    return a, b, c


def _proj_kernel(x_ref, g_ref, w_ref, o_ref, xn_ref, *, per_head):
    j = pl.program_id(1)

    @pl.when(j == 0)
    def _():
        xn_ref[...] = _rmsnorm(x_ref[...], g_ref[...]).astype(BF16)

    res = jnp.dot(xn_ref[...], w_ref[...], preferred_element_type=F32)
    if per_head:
        res = res * jnp.where((j == 0) | (j == 3), HEAD_DIM ** -0.5, 1.0)
        for h in range(N_HEADS):
            o_ref[h] = res[:, h * LANES:(h + 1) * LANES].astype(BF16)
    else:
        o_ref[...] = jax.nn.sigmoid(res).astype(BF16)


def _proj(x2d, g, w, *, per_head, tm):
    T, D = x2d.shape
    ncol = w.shape[1] // 1024
    if per_head:
        out_shape = jax.ShapeDtypeStruct((ncol * N_HEADS, T, LANES), BF16)
        out_spec = pl.BlockSpec((N_HEADS, tm, LANES), lambda i, j: (j, i, 0))
    else:
        out_shape = jax.ShapeDtypeStruct((T, w.shape[1]), BF16)
        out_spec = pl.BlockSpec((tm, 1024), lambda i, j: (i, j))
    return pl.pallas_call(
        functools.partial(_proj_kernel, per_head=per_head),
        grid=(T // tm, ncol),
        in_specs=[pl.BlockSpec((tm, D), lambda i, j: (i, 0)),
                  pl.BlockSpec((1, D), lambda i, j: (0, 0)),
                  pl.BlockSpec((D, 1024), lambda i, j: (0, j))],
        out_specs=out_spec,
        out_shape=out_shape,
        scratch_shapes=[pltpu.VMEM((tm, D), BF16)],
        compiler_params=_cparams(("arbitrary", "arbitrary")),
        name="proj_heads" if per_head else "proj_gates",
    )(x2d, g, w)


def _moba_kernel(slope_ref, q_ref, k_ref, v_ref, o_ref, kaug_ref, vaug_ref, kmean_ref, *, nb, groups):
    g = pl.program_id(0)
    i = pl.program_id(1)
    hg, tq, _ = q_ref.shape
    blk = MOBA_BLOCK
    head0 = (g % groups) * hg

    @pl.when(i == 0)
    def _():
        kmean_ref[...] = jnp.zeros_like(kmean_ref)
        row = lax.broadcasted_iota(jnp.int32, (blk, LANES), 0)
        lane = lax.broadcasted_iota(jnp.int32, (blk, LANES), 1)
        ones_q = jnp.where((lane >= LANE_QPOS) & (lane < LANE_QPOS + 3), 1.0, 0.0)
        for hh in range(hg):
            slope = slope_ref[head0 + hh]

            def fill(n, c, hh=hh, slope=slope):
                sl = pl.ds(pl.multiple_of(n * blk, blk), blk)
                kf = k_ref[hh, sl, :].astype(F32)
                kmean_ref[hh, pl.ds(n, 1), :] = jnp.mean(kf, axis=0, keepdims=True)
                pa, pb, pc = _split3(slope * (n * blk + row).astype(F32))
                feat = jnp.where(lane == LANE_BLK0 + n, 1.0, ones_q)
                feat = jnp.where(lane == LANE_KPOS, pa, feat)
                feat = jnp.where(lane == LANE_KPOS + 1, pb, feat)
                feat = jnp.where(lane == LANE_KPOS + 2, pc, feat)
                kaug_ref[hh, sl, :] = jnp.where(lane < HEAD_DIM, kf, feat).astype(BF16)
                vf = v_ref[hh, sl, :].astype(F32)
                vaug_ref[hh, sl, :] = jnp.where(lane == HEAD_DIM, 1.0, vf).astype(BF16)
                return c

            lax.fori_loop(0, nb, fill, 0)

    heads = range(hg)
    lane = lax.broadcasted_iota(jnp.int32, (tq, LANES), 1)
    is_blk_lane = (lane >= LANE_BLK0) & (lane < LANE_BLK0 + MAX_KV_BLOCKS)
    rowb = lax.broadcasted_iota(jnp.int32, (MAX_KV_BLOCKS, tq), 0)
    tpos = (i * blk + lax.broadcasted_iota(jnp.int32, (tq, LANES), 0)).astype(F32)
    own_sl = pl.ds(pl.multiple_of(i * blk, blk), blk)
    r = lax.broadcasted_iota(jnp.int32, (tq, blk), 0)
    c = lax.broadcasted_iota(jnp.int32, (tq, blk), 1)
    causal = c <= r

    qfs = [q_ref[hh].astype(F32) for hh in heads]
    gate = [lax.dot_general(kmean_ref[hh], qfs[hh], NT_DIMS, precision=lax.Precision.HIGHEST,
                            preferred_element_type=F32) for hh in heads]
    qowns, own_scores = [], []
    for hh in heads:
        pa, pb, pc = _split3(-slope_ref[head0 + hh] * tpos)
        feat = jnp.where(lane == LANE_QPOS, pa, jnp.where(lane == LANE_QPOS + 1, pb,
                         jnp.where(lane == LANE_QPOS + 2, pc,
                                   jnp.where((lane >= LANE_KPOS) & (lane < LANE_KPOS + 3), 1.0, 0.0))))
        qown = jnp.where(lane < HEAD_DIM, qfs[hh], feat).astype(BF16)
        qowns.append(qown)
        own_scores.append(lax.dot_general(qown, kaug_ref[hh, own_sl, :], NT_DIMS,
                                          preferred_element_type=F32))
    qaugs = []
    for hh in heads:
        bsm = jnp.where(rowb < i, gate[hh], -jnp.inf)
        sel = jnp.zeros(bsm.shape, jnp.bool_)
        for _ in range(MOBA_TOPK):
            m = jnp.max(bsm, axis=0, keepdims=True)
            p = jnp.min(jnp.where(bsm == m, rowb, MAX_KV_BLOCKS), axis=0, keepdims=True)
            hit = rowb == p
            sel = sel | (hit & (m > -jnp.inf))
            bsm = jnp.where(hit, -jnp.inf, bsm)
        bias_t = jnp.where(sel, 0.0, NEG_BIG)
        bias = jnp.concatenate([jnp.zeros((LANE_BLK0, tq), F32), bias_t,
                                jnp.zeros((LANES - LANE_QPOS, tq), F32)], axis=0).T
        qaugs.append(jnp.where(is_blk_lane, bias.astype(BF16), qowns[hh]))
    states = []
    for hh in heads:
        s0 = jnp.where(causal, own_scores[hh], NEG_BIG)
        m0 = jnp.max(s0, axis=1, keepdims=True)
        p0 = jnp.exp(s0 - m0)
        acc0 = jnp.dot(p0.astype(BF16), vaug_ref[hh, own_sl, :], preferred_element_type=F32)
        states.append((m0, acc0))

    span = MOBA_KV_CHUNK * blk

    def body(cidx, states):
        sl = pl.ds(pl.multiple_of(cidx * span, span), span)
        scores = [lax.dot_general(qaugs[hh], kaug_ref[hh, sl, :], NT_DIMS, preferred_element_type=F32)
                  for hh in range(hg)]
        new = []
        for hh in range(hg):
            m, acc = states[hh]
            s = scores[hh]
            mn = jnp.maximum(m, jnp.max(s, axis=1, keepdims=True))
            alpha = jnp.exp(m - mn)
            p = jnp.exp(s - mn)
            acc = alpha * acc + jnp.dot(p.astype(BF16), vaug_ref[hh, sl, :], preferred_element_type=F32)
            new.append((mn, acc))
        return tuple(new)

    n_chunks = (i + MOBA_KV_CHUNK - 1) // MOBA_KV_CHUNK
    states = lax.fori_loop(0, n_chunks, body, tuple(states))
    for hh in range(hg):
        acc = states[hh][1]
        denom = acc[:, HEAD_DIM:HEAD_DIM + 1]
        o_ref[hh] = jnp.where(lane < HEAD_DIM, acc / denom, 0.0).astype(BF16)


def _moba(qkv, slopes, B, S):
    T = B * S
    nq = S // MOBA_BLOCK
    nb = S // MOBA_BLOCK
    assert nb <= MAX_KV_BLOCKS and nb % MOBA_KV_CHUNK == 0
    H = N_HEADS
    hg = MOBA_HEADS_PER_STEP
    G = H // hg
    return pl.pallas_call(
        functools.partial(_moba_kernel, nb=nb, groups=G),
        grid=(B * G, nq),
        in_specs=[pl.BlockSpec(memory_space=pltpu.SMEM),
                  pl.BlockSpec((hg, MOBA_BLOCK, LANES), lambda g, i: (g % G, (g // G) * nq + i, 0)),
                  pl.BlockSpec((hg, S, LANES), lambda g, i: (G + g % G, g // G, 0)),
                  pl.BlockSpec((hg, S, LANES), lambda g, i: (2 * G + g % G, g // G, 0))],
        out_specs=pl.BlockSpec((hg, MOBA_BLOCK, LANES), lambda g, i: (g % G, (g // G) * nq + i, 0)),
        out_shape=jax.ShapeDtypeStruct((H, T, LANES), BF16),
        scratch_shapes=[pltpu.VMEM((hg, S, LANES), BF16), pltpu.VMEM((hg, S, LANES), BF16),
                        pltpu.VMEM((hg, MAX_KV_BLOCKS, LANES), F32)],
        compiler_params=_cparams(("arbitrary", "arbitrary")),
        name="moba_attn",
    )(slopes, qkv, qkv, qkv)


def _sb_kernel(q_ref, k_ref, v_ref, tri_ref, o_ref):
    i = pl.program_id(1)
    blk = MOBA_BLOCK
    hg, tq, _ = q_ref.shape
    tri = tri_ref[...]

    def tiles(j, states, diag):
        heads = range(hg)
        sl = pl.ds(pl.multiple_of(j * blk, blk), blk)
        zs = [lax.dot_general(q_ref[hh], k_ref[hh, sl, :], NT_DIMS, preferred_element_type=F32)
              for hh in heads]
        if diag:
            r = lax.broadcasted_iota(jnp.int32, (tq, blk), 0)
            c = lax.broadcasted_iota(jnp.int32, (tq, blk), 1)
            strict = c < r
        cums = []
        for hh in heads:
            z = zs[hh]
            lg = jnp.minimum(-z, 0.0) - jnp.log(1.0 + jnp.exp(-jnp.abs(z)))
            if diag:
                lg = jnp.where(strict, lg, 0.0)
            hi, lo = _split_hi_lo(lg)
            cums.append(jnp.dot(hi, tri, preferred_element_type=F32)
                        + jnp.dot(lo, tri, preferred_element_type=F32))
        new = []
        for hh in heads:
            carry, decay, acc = states[hh]
            w = jnp.exp(zs[hh] + cums[hh])
            if diag:
                w = jnp.where(strict, w, 0.0)
            pv = jnp.dot(w.astype(BF16), v_ref[hh, sl, :], preferred_element_type=F32)
            carry = carry + cums[hh][:, 0:1]
            new.append((carry, jnp.exp(carry), acc + decay * pv))
        return tuple(new)

    def any_alive(states):
        alive = jnp.max(states[0][1])
        for st in states[1:]:
            alive = jnp.maximum(alive, jnp.max(st[1]))
        return alive > 0.0

    init = (jnp.zeros((tq, 1), F32), jnp.ones((tq, 1), F32), jnp.zeros((tq, LANES), F32))
    states = tiles(i, (init,) * hg, True)

    def cond(c):
        t, alive, _ = c
        return (t < i) & alive

    def body(c):
        t, _, states = c
        states = tiles(i - 1 - t, states, False)
        return t + 1, any_alive(states), states

    _, _, states = lax.while_loop(cond, body, (jnp.int32(0), any_alive(states), states))
    for hh in range(hg):
        o_ref[hh] = states[hh][2].astype(BF16)


def _sb(qkv, tri, B, S):
    T = B * S
    nq = S // MOBA_BLOCK
    H = N_HEADS
    hg = SB_HEADS_PER_STEP
    G = H // hg
    return pl.pallas_call(
        _sb_kernel,
        grid=(B * G, nq),
        in_specs=[pl.BlockSpec((hg, MOBA_BLOCK, LANES), lambda g, i: (3 * G + g % G, (g // G) * nq + i, 0)),
                  pl.BlockSpec((hg, S, LANES), lambda g, i: (4 * G + g % G, g // G, 0)),
                  pl.BlockSpec((hg, S, LANES), lambda g, i: (5 * G + g % G, g // G, 0)),
                  pl.BlockSpec((MOBA_BLOCK, MOBA_BLOCK), lambda g, i: (0, 0))],
        out_specs=pl.BlockSpec((hg, MOBA_BLOCK, LANES), lambda g, i: (g % G, (g // G) * nq + i, 0)),
        out_shape=jax.ShapeDtypeStruct((H, T, LANES), BF16),
        compiler_params=_cparams(("arbitrary", "arbitrary")),
        name="sb_attn",
    )(qkv, qkv, qkv, tri)


def _mix_kernel(x_ref, oa_ref, ob_ref, gate_ref, woa_ref, wob_ref, wmix_ref, g2_ref, wq_ref,
                h_ref, x2_ref, qp_ref):
    D = x_ref.shape[1]

    def out_proj(o_ref, w_ref):
        y = None
        for p in range(N_HEADS // 2):
            pair = jnp.concatenate([o_ref[2 * p], o_ref[2 * p + 1]], axis=1)
            part = jnp.dot(pair, w_ref[p], preferred_element_type=F32)
            y = part if y is None else y + part
        return y

    ya = out_proj(oa_ref, woa_ref)
    yb = out_proj(ob_ref, wob_ref)
    mixed = gate_ref[:, :D].astype(F32) * ya + gate_ref[:, D:].astype(F32) * yb
    h = x_ref[...] + jnp.dot(mixed.astype(BF16), wmix_ref[...], preferred_element_type=F32)
    h_ref[...] = h
    x2 = _rmsnorm(h, g2_ref[...]).astype(BF16)
    x2_ref[...] = x2
    qp_ref[...] = jnp.dot(x2, wq_ref[...], preferred_element_type=F32)


def _mix(x2d, oa, ob, gates, woa, wob, wmix, g2, wq, *, tm):
    T, D = x2d.shape
    nqp = wq.shape[1]
    H = N_HEADS
    const = lambda *shape: pl.BlockSpec(shape, lambda i: (0,) * len(shape))
    return pl.pallas_call(
        _mix_kernel,
        grid=(T // tm,),
        in_specs=[pl.BlockSpec((tm, D), lambda i: (i, 0)),
                  pl.BlockSpec((H, tm, LANES), lambda i: (0, i, 0)),
                  pl.BlockSpec((H, tm, LANES), lambda i: (0, i, 0)),
                  pl.BlockSpec((tm, 2 * D), lambda i: (i, 0)),
                  const(H // 2, 2 * LANES, D), const(H // 2, 2 * LANES, D), const(D, D), const(1, D),
                  const(D, nqp)],
        out_specs=[pl.BlockSpec((tm, D), lambda i: (i, 0)),
                   pl.BlockSpec((tm, D), lambda i: (i, 0)),
                   pl.BlockSpec((tm, nqp), lambda i: (i, 0))],
        out_shape=[jax.ShapeDtypeStruct((T, D), F32),
                   jax.ShapeDtypeStruct((T, D), BF16),
                   jax.ShapeDtypeStruct((T, nqp), F32)],
        compiler_params=_cparams(("arbitrary",)),
        name="mix_out",
    )(x2d, oa, ob, gates, woa, wob, wmix, g2, wq)


_CAND_SLABS = ((0, 16, 0), (1, 8, 16), (2, 5, 24), (3, 4, 32), (4, 3, 40), (5, 2, 48), (6, 2, 56), (7, 2, 64))
_CAND_TAIL_ROW0 = 72
_CAND_ROWS = 80


def _topk_rank(x, k, top_ref, *, tie_break, track_rank=True):
    R, n = x.shape
    row = lax.broadcasted_iota(jnp.int32, (R, n), 0)
    rank = jnp.full((R, n), k, jnp.int32) if track_rank else None
    vals = []
    for r in range(k):
        m = jnp.max(x, axis=0, keepdims=True)
        hit = x == m
        if tie_break:
            hit = row == jnp.min(jnp.where(hit, row, R), axis=0, keepdims=True)
        if track_rank:
            rank = jnp.where(hit, r, rank)
        x = jnp.where(hit, -jnp.inf, x)
        vals.append(m)
        if top_ref is not None:
            top_ref[r:r + 1, :] = m
    taken = (rank < k) if track_rank else (x == -jnp.inf)
    count = jnp.sum(taken.astype(F32), axis=0, keepdims=True)
    return rank, vals, count


def _route_select(s1, s2, write, top1_ref, top2_ref, cand_ref, *, tie_break):
    K = PEER_TOPK
    n = s1.shape[1]
    rank1, _, n1 = _topk_rank(s1, K, top1_ref, tie_break=tie_break, track_rank=tie_break)
    rank2, _, n2 = _topk_rank(s2, K, top2_ref, tie_break=tie_break)
    t1 = top1_ref[...]
    t2 = top2_ref[...]
    row8 = lax.broadcasted_iota(jnp.int32, (8, n), 0)
    for a, nvalid, off in _CAND_SLABS:
        rows = 16 if nvalid == 16 else 8
        slab = t1[a:a + 1, :] + t2[0:rows, :]
        if nvalid < rows:
            slab = jnp.where(row8 < nvalid, slab, -jnp.inf)
        cand_ref[off:off + rows, :] = slab
    cand_ref[_CAND_TAIL_ROW0:_CAND_ROWS, :] = t1[8:16, :] + t2[0:1, :]

    crank, cvals, nc = _topk_rank(cand_ref[...], K, None, tie_break=tie_break)
    sel = (crank < K).astype(F32)
    z = jnp.zeros((1, n), F32)
    for r in range(K):
        z = z + jnp.exp(cvals[r] - cvals[0])
    lr = jnp.zeros(s1.shape, F32)
    for a in range(K):
        if a < 8:
            _, nvalid, off = _CAND_SLABS[a]
            rows = 16 if nvalid == 16 else 8
            cnt = jnp.sum(sel[off:off + rows, :], axis=0, keepdims=True)
        else:
            cnt = sel[_CAND_TAIL_ROW0 + a - 8:_CAND_TAIL_ROW0 + a - 7, :]
        lr = jnp.where((rank1 == a) if tie_break else (s1 == t1[a:a + 1, :]), cnt, lr)
    write(lr, jnp.exp(s1 - t1[0:1, :]) / z, rank2.astype(F32), jnp.exp(s2 - t2[0:1, :]))
    excess = jnp.maximum(jnp.maximum(jnp.abs(n1 - K), jnp.abs(n2 - K)), jnp.abs(nc - K))
    return jnp.max(excess) == 0.0


BF16_SUBLANES = 16
PEER_ROWBLOCKS_PER_DOT = 2


def _bcast_rows_bf16(row, n):
    tile = jnp.broadcast_to(row, (BF16_SUBLANES, row.shape[1])).astype(BF16)
    return jnp.concatenate([tile] * (n // BF16_SUBLANES), axis=0)


def _peer_kernel(qp_ref, sk_ref, x2_ref, h_ref, u_ref, vt_ref, gf_ref, o_ref,
                 acc_ref, cnt_s, e1_s, r2p_s, e2p_s, top1_ref, top2_ref, cand_ref,
                 *, n_i, n_tiles, final_norm):
    i = pl.program_id(0)
    e = pl.program_id(1)
    last = pl.num_programs(1) - 1
    wslot = i % 2
    rslot = 1 - wslot
    NK = PEER_NKEYS
    tt = x2_ref.shape[0]

    def scores():
        q = qp_ref[...]
        half = q.shape[1] // 2
        s1 = lax.dot_general(sk_ref[0, 0], q[:, :half], NT_DIMS, precision=lax.Precision.HIGHEST,
                             preferred_element_type=F32)
        s2 = lax.dot_general(sk_ref[0, 1], q[:, half:], NT_DIMS, precision=lax.Precision.HIGHEST,
                             preferred_element_type=F32)
        return s1, s2

    def write(cnt, e1, rank2, e2):
        cnt_s[wslot, e] = cnt
        e1_s[wslot, e] = e1
        r2p_s[wslot, e] = pltpu.bitcast(rank2.astype(BF16), jnp.uint32)
        e2p_s[wslot, e] = pltpu.bitcast(e2.astype(BF16), jnp.uint32)

    def route(s1, s2, tie_break):
        return _route_select(s1, s2, write, top1_ref, top2_ref, cand_ref, tie_break=tie_break)

    def route_exact_if_tied(s1, s2, tie_free):
        @pl.when(jnp.logical_not(tie_free))
        def _():
            route(s1, s2, True)

    def experts():
        zero = jnp.zeros((NK, LANES), BF16)
        cnts = [[cnt_s[rslot, hh, pl.ds(e * n_i + ii, 1), :] for hh in range(PEER_HEADS)]
                for ii in range(n_i)]
        e1s = [[e1_s[rslot, hh, pl.ds(e * n_i + ii, 1), :] for hh in range(PEER_HEADS)]
               for ii in range(n_i)]
        x2 = x2_ref[...]
        w_rows = []
        for ii in range(n_i):
            if ii % PEER_ROWBLOCKS_PER_DOT == 0:
                r0 = ii * NK
                at = lax.dot_general(u_ref[r0:r0 + PEER_ROWBLOCKS_PER_DOT * NK, :], x2, NT_DIMS,
                                     preferred_element_type=F32)
            ar0 = (ii % PEER_ROWBLOCKS_PER_DOT) * NK
            w_cols = []
            for c0 in range(0, tt, LANES):
                cols = slice(c0, c0 + LANES)
                a = at[ar0:ar0 + NK, cols]
                ga = (a + a * lax.erf(a)).astype(BF16)
                coef = None
                for hh in range(PEER_HEADS):
                    cnt = _bcast_rows_bf16(cnts[ii][hh][:, cols], NK)
                    e1 = _bcast_rows_bf16(e1s[ii][hh][:, cols], NK)
                    r2 = pltpu.bitcast(r2p_s[rslot, hh, :, cols], BF16)
                    e2 = pltpu.bitcast(e2p_s[rslot, hh, :, cols], BF16)
                    term = jnp.where(r2 < cnt, e2, zero) * e1
                    coef = term if coef is None else coef + term
                w_cols.append(coef * ga)
            w_rows.append(jnp.concatenate(w_cols, axis=1))
        w = jnp.concatenate(w_rows, axis=0)
        acc_ref[...] += lax.dot_general(vt_ref[...], w, (((0,), (0,)), ((), ())),
                                        preferred_element_type=F32)

    def finish_tile():
        @pl.when(e == last)
        def _():
            y = h_ref[...] + acc_ref[...].T
            o_ref[...] = _rmsnorm(y, gf_ref[...]) if final_norm else y
            acc_ref[...] = jnp.zeros_like(acc_ref)

    @pl.when(i == 0)
    def _():
        @pl.when(e == 0)
        def _():
            acc_ref[...] = jnp.zeros_like(acc_ref)

        s1, s2 = scores()
        route_exact_if_tied(s1, s2, route(s1, s2, False))

    @pl.when((i > 0) & (i < n_tiles))
    def _():
        s1, s2 = scores()
        tie_free = route(s1, s2, False)
        experts()
        route_exact_if_tied(s1, s2, tie_free)
        finish_tile()

    @pl.when(i == n_tiles)
    def _():
        experts()
        finish_tile()


def _peer(qp, sub_keys, x2, h, u, vt, gf, *, tt, eb, final_norm):
    T, D = h.shape
    NE = u.shape[0]
    PH, NK = PEER_HEADS, PEER_NKEYS
    n_tiles = T // tt
    assert NE // eb == PH
    qd = qp.shape[1] // PH
    prev = lambda i: jnp.maximum(i - 1, 0)
    blk = lambda i, e: jnp.where(i == 0, 0, e)
    return pl.pallas_call(
        functools.partial(_peer_kernel, n_i=eb // NK, n_tiles=n_tiles, final_norm=final_norm),
        grid=(n_tiles + 1, PH),
        in_specs=[pl.BlockSpec((tt, qd), lambda i, e: (jnp.minimum(i, n_tiles - 1), e)),
                  pl.BlockSpec((1, 2, NK, qd // 2), lambda i, e: (e, 0, 0, 0)),
                  pl.BlockSpec((tt, D), lambda i, e: (prev(i), 0)),
                  pl.BlockSpec((tt, D), lambda i, e: (prev(i), 0)),
                  pl.BlockSpec((eb, D), lambda i, e: (blk(i, e), 0)),
                  pl.BlockSpec((eb, D), lambda i, e: (blk(i, e), 0)),
                  pl.BlockSpec((1, D), lambda i, e: (0, 0))],
        out_specs=pl.BlockSpec((tt, D), lambda i, e: (prev(i), 0)),
        out_shape=jax.ShapeDtypeStruct((T, D), F32),
        scratch_shapes=[pltpu.VMEM((D, tt), F32),
                        pltpu.VMEM((2, PH, NK, tt), F32), pltpu.VMEM((2, PH, NK, tt), F32),
                        pltpu.VMEM((2, PH, NK // 2, tt), jnp.uint32),
                        pltpu.VMEM((2, PH, NK // 2, tt), jnp.uint32),
                        pltpu.VMEM((PEER_TOPK, tt), F32), pltpu.VMEM((PEER_TOPK, tt), F32),
                        pltpu.VMEM((_CAND_ROWS, tt), F32)],
        compiler_params=_cparams(("arbitrary", "arbitrary")),
        name="peer_route_experts",
    )(qp, sub_keys, x2, h, u, vt, gf)


def _pad_heads_cols(w):
    D, n = w.shape
    w = w.reshape(D, n // HEAD_DIM, HEAD_DIM)
    w = jnp.pad(w, ((0, 0), (0, 0), (0, LANES - HEAD_DIM)))
    return w.reshape(D, (n // HEAD_DIM) * LANES)


def _pad_heads_rows(w):
    n, D = w.shape
    w = w.reshape(n // HEAD_DIM, HEAD_DIM, D)
    w = jnp.pad(w, ((0, 0), (0, LANES - HEAD_DIM), (0, 0)))
    return w.reshape(n // HEAD_DIM // 2, 2 * LANES, D)


def kernel(x, norm1_g, w_in, w_out_moba, w_out_sb, w_mix_out, norm2_g, peer_w_q, peer_sub_keys, peer_u,
           peer_v, final_norm_g):
    B, S, D = x.shape
    T = B * S
    depth = w_in.shape[0]
    n_qkv = 6 * N_HEADS * HEAD_DIM
    tm = min(1024, T)
    slopes = jnp.exp2(-8.0 * jnp.arange(1, N_HEADS + 1, dtype=F32) / N_HEADS)
    tri = jnp.tril(jnp.ones((MOBA_BLOCK, MOBA_BLOCK), F32)).astype(BF16)

    h = x.reshape(T, D)
    out = h
    for l in range(depth):
        w_heads = _pad_heads_cols(w_in[l][:, :n_qkv]).astype(BF16)
        w_gates = w_in[l][:, n_qkv:].astype(BF16)
        g1 = norm1_g[l].reshape(1, D)
        qkv = _proj(h, g1, w_heads, per_head=True, tm=tm)
        gates = _proj(h, g1, w_gates, per_head=False, tm=tm)
        oa = _moba(qkv, slopes, B, S)
        ob = _sb(qkv, tri, B, S)
        h, x2, qp = _mix(h, oa, ob, gates,
                         _pad_heads_rows(w_out_moba[l]).astype(BF16),
                         _pad_heads_rows(w_out_sb[l]).astype(BF16),
                         w_mix_out[l].astype(BF16), norm2_g[l].reshape(1, D),
                         peer_w_q[l].astype(BF16), tm=min(256, T))
        out = _peer(qp, peer_sub_keys[l], x2, h, (peer_u[l] * INV_SQRT2).astype(BF16),
                    (peer_v[l] * INV_SQRT2).astype(BF16),
                    final_norm_g.reshape(1, D), tt=min(512, T), eb=2048, final_norm=(l == depth - 1))
        h = out
    return out.reshape(B, S, D)
```

```python
import functools

import jax
import jax.numpy as jnp
from jax import lax
from jax.experimental import pallas as pl
from jax.experimental.pallas import tpu as pltpu

F32 = jnp.float32
BF16 = jnp.bfloat16

HEAD_DIM = 64
N_HEADS = 8
LANES = 128
MOBA_BLOCK = 256
MOBA_TOPK = 3
PEER_HEADS = 8
PEER_NKEYS = 128
PEER_TOPK = 16
RMS_EPS = 1e-6
NEG_BIG = -1e30
INV_SQRT2 = 0.7071067811865476
VMEM_LIMIT = 56 * 1024 * 1024

LANE_BLK0 = 64
MAX_KV_BLOCKS = 32
LANE_QPOS = 96
LANE_KPOS = 99
MOBA_HEADS_PER_STEP = 4
MOBA_KV_CHUNK = 8
SB_HEADS_PER_STEP = 4

PROJ_COLS = N_HEADS * LANES

NT_DIMS = (((1,), (1,)), ((), ()))


def _cparams(sem):
    return pltpu.CompilerParams(dimension_semantics=sem, vmem_limit_bytes=VMEM_LIMIT)


def _rmsnorm(x, g):
    return x * lax.rsqrt(jnp.mean(x * x, axis=-1, keepdims=True) + RMS_EPS) * g


def _split_hi_lo(x):
    hi = x.astype(BF16)
    lo = (x - hi.astype(F32)).astype(BF16)
    return hi, lo


def _split3(x):
    a = x.astype(BF16).astype(F32)
    r = x - a
    b = r.astype(BF16).astype(F32)
    c = (r - b).astype(BF16).astype(F32)
    return a, b, c


def _proj_kernel(x_ref, g_ref, w_ref, o_ref, xn_ref, *, per_head):
    j = pl.program_id(1)

    @pl.when(j == 0)
    def _():
        xn_ref[...] = _rmsnorm(x_ref[...], g_ref[...]).astype(BF16)

    res = jnp.dot(xn_ref[...], w_ref[...], preferred_element_type=F32)
    if per_head:
        res = res * jnp.where((j == 0) | (j == 3), HEAD_DIM ** -0.5, 1.0)
        for h in range(N_HEADS):
            o_ref[h] = res[:, h * LANES:(h + 1) * LANES].astype(BF16)
    else:
        o_ref[...] = jax.nn.sigmoid(res).astype(BF16)


def _proj(x2d, g, w, *, per_head, tm):
    T, D = x2d.shape
    ncol = w.shape[1] // PROJ_COLS
    if per_head:
        out_shape = jax.ShapeDtypeStruct((ncol * N_HEADS, T, LANES), BF16)
        out_spec = pl.BlockSpec((N_HEADS, tm, LANES), lambda i, j: (j, i, 0))
    else:
        out_shape = jax.ShapeDtypeStruct((T, w.shape[1]), BF16)
        out_spec = pl.BlockSpec((tm, PROJ_COLS), lambda i, j: (i, j))
    return pl.pallas_call(
        functools.partial(_proj_kernel, per_head=per_head),
        grid=(T // tm, ncol),
        in_specs=[pl.BlockSpec((tm, D), lambda i, j: (i, 0)),
                  pl.BlockSpec((1, D), lambda i, j: (0, 0)),
                  pl.BlockSpec((D, PROJ_COLS), lambda i, j: (0, j))],
        out_specs=out_spec,
        out_shape=out_shape,
        scratch_shapes=[pltpu.VMEM((tm, D), BF16)],
        compiler_params=_cparams(("arbitrary", "arbitrary")),
        name="proj_heads" if per_head else "proj_gates",
    )(x2d, g, w)


def _moba_kernel(slope_ref, q_ref, k_ref, v_ref, o_ref, kaug_ref, vaug_ref, kmean_ref, *, nb, groups):
    g = pl.program_id(0)
    i = pl.program_id(1)
    hg, tq, _ = q_ref.shape
    blk = MOBA_BLOCK
    head0 = (g % groups) * hg

    @pl.when(i == 0)
    def _():
        kmean_ref[...] = jnp.zeros_like(kmean_ref)
        row = lax.broadcasted_iota(jnp.int32, (blk, LANES), 0)
        lane = lax.broadcasted_iota(jnp.int32, (blk, LANES), 1)
        ones_q = jnp.where((lane >= LANE_QPOS) & (lane < LANE_QPOS + 3), 1.0, 0.0)
        for hh in range(hg):
            slope = slope_ref[head0 + hh]

            def fill(n, c, hh=hh, slope=slope):
                sl = pl.ds(pl.multiple_of(n * blk, blk), blk)
                kf = k_ref[hh, sl, :].astype(F32)
                kmean_ref[hh, pl.ds(n, 1), :] = jnp.mean(kf, axis=0, keepdims=True)
                pa, pb, pc = _split3(slope * (n * blk + row).astype(F32))
                feat = jnp.where(lane == LANE_BLK0 + n, 1.0, ones_q)
                feat = jnp.where(lane == LANE_KPOS, pa, feat)
                feat = jnp.where(lane == LANE_KPOS + 1, pb, feat)
                feat = jnp.where(lane == LANE_KPOS + 2, pc, feat)
                kaug_ref[hh, sl, :] = jnp.where(lane < HEAD_DIM, kf, feat).astype(BF16)
                vf = v_ref[hh, sl, :].astype(F32)
                vaug_ref[hh, sl, :] = jnp.where(lane == HEAD_DIM, 1.0, vf).astype(BF16)
                return c

            lax.fori_loop(0, nb, fill, 0)

    heads = range(hg)
    lane = lax.broadcasted_iota(jnp.int32, (tq, LANES), 1)
    is_blk_lane = (lane >= LANE_BLK0) & (lane < LANE_BLK0 + MAX_KV_BLOCKS)
    rowb = lax.broadcasted_iota(jnp.int32, (MAX_KV_BLOCKS, tq), 0)
    tpos = (i * blk + lax.broadcasted_iota(jnp.int32, (tq, LANES), 0)).astype(F32)
    own_sl = pl.ds(pl.multiple_of(i * blk, blk), blk)
    r = lax.broadcasted_iota(jnp.int32, (tq, blk), 0)
    c = lax.broadcasted_iota(jnp.int32, (tq, blk), 1)
    causal = c <= r

    qfs = [q_ref[hh].astype(F32) for hh in heads]
    gate = [lax.dot_general(kmean_ref[hh], qfs[hh], NT_DIMS, precision=lax.Precision.HIGHEST,
                            preferred_element_type=F32) for hh in heads]
    qowns, own_scores = [], []
    for hh in heads:
        pa, pb, pc = _split3(-slope_ref[head0 + hh] * tpos)
        feat = jnp.where(lane == LANE_QPOS, pa, jnp.where(lane == LANE_QPOS + 1, pb,
                         jnp.where(lane == LANE_QPOS + 2, pc,
                                   jnp.where((lane >= LANE_KPOS) & (lane < LANE_KPOS + 3), 1.0, 0.0))))
        qown = jnp.where(lane < HEAD_DIM, qfs[hh], feat).astype(BF16)
        qowns.append(qown)
        own_scores.append(lax.dot_general(qown, kaug_ref[hh, own_sl, :], NT_DIMS,
                                          preferred_element_type=F32))
    qaugs = []
    for hh in heads:
        bsm = jnp.where(rowb < i, gate[hh], -jnp.inf)
        sel = jnp.zeros(bsm.shape, jnp.bool_)
        for _ in range(MOBA_TOPK):
            m = jnp.max(bsm, axis=0, keepdims=True)
            p = jnp.min(jnp.where(bsm == m, rowb, MAX_KV_BLOCKS), axis=0, keepdims=True)
            hit = rowb == p
            sel = sel | (hit & (m > -jnp.inf))
            bsm = jnp.where(hit, -jnp.inf, bsm)
        bias_t = jnp.where(sel, 0.0, NEG_BIG)
        bias = jnp.concatenate([jnp.zeros((LANE_BLK0, tq), F32), bias_t,
                                jnp.zeros((LANES - LANE_QPOS, tq), F32)], axis=0).T
        qaugs.append(jnp.where(is_blk_lane, bias.astype(BF16), qowns[hh]))
    states = []
    for hh in heads:
        s0 = jnp.where(causal, own_scores[hh], NEG_BIG)
        m0 = jnp.max(s0, axis=1, keepdims=True)
        p0 = jnp.exp(s0 - m0)
        acc0 = jnp.dot(p0.astype(BF16), vaug_ref[hh, own_sl, :], preferred_element_type=F32)
        states.append((m0, acc0))

    span = MOBA_KV_CHUNK * blk

    def body(cidx, states):
        sl = pl.ds(pl.multiple_of(cidx * span, span), span)
        scores = [lax.dot_general(qaugs[hh], kaug_ref[hh, sl, :], NT_DIMS, preferred_element_type=F32)
                  for hh in range(hg)]
        new = []
        for hh in range(hg):
            m, acc = states[hh]
            s = scores[hh]
            mn = jnp.maximum(m, jnp.max(s, axis=1, keepdims=True))
            alpha = jnp.exp(m - mn)
            p = jnp.exp(s - mn)
            acc = alpha * acc + jnp.dot(p.astype(BF16), vaug_ref[hh, sl, :], preferred_element_type=F32)
            new.append((mn, acc))
        return tuple(new)

    n_chunks = (i + MOBA_KV_CHUNK - 1) // MOBA_KV_CHUNK
    states = lax.fori_loop(0, n_chunks, body, tuple(states))
    for hh in range(hg):
        acc = states[hh][1]
        denom = acc[:, HEAD_DIM:HEAD_DIM + 1]
        o_ref[hh] = jnp.where(lane < HEAD_DIM, acc / denom, 0.0).astype(BF16)


def _moba(qkv, slopes, B, S):
    T = B * S
    nq = S // MOBA_BLOCK
    nb = S // MOBA_BLOCK
    assert nb <= MAX_KV_BLOCKS and nb % MOBA_KV_CHUNK == 0
    H = N_HEADS
    hg = MOBA_HEADS_PER_STEP
    G = H // hg
    return pl.pallas_call(
        functools.partial(_moba_kernel, nb=nb, groups=G),
        grid=(B * G, nq),
        in_specs=[pl.BlockSpec(memory_space=pltpu.SMEM),
                  pl.BlockSpec((hg, MOBA_BLOCK, LANES), lambda g, i: (g % G, (g // G) * nq + i, 0)),
                  pl.BlockSpec((hg, S, LANES), lambda g, i: (G + g % G, g // G, 0),
                               pipeline_mode=pl.Buffered(1)),
                  pl.BlockSpec((hg, S, LANES), lambda g, i: (2 * G + g % G, g // G, 0),
                               pipeline_mode=pl.Buffered(1))],
        out_specs=pl.BlockSpec((hg, MOBA_BLOCK, LANES), lambda g, i: (g % G, (g // G) * nq + i, 0)),
        out_shape=jax.ShapeDtypeStruct((H, T, LANES), BF16),
        scratch_shapes=[pltpu.VMEM((hg, S, LANES), BF16), pltpu.VMEM((hg, S, LANES), BF16),
                        pltpu.VMEM((hg, MAX_KV_BLOCKS, LANES), F32)],
        compiler_params=_cparams(("arbitrary", "arbitrary")),
        name="moba_attn",
    )(slopes, qkv, qkv, qkv)


def _sb_kernel(q_ref, k_ref, v_ref, tri_ref, o_ref):
    i = pl.program_id(1)
    blk = MOBA_BLOCK
    hg, tq, _ = q_ref.shape
    tri = tri_ref[...]

    def tiles(j, states, diag):
        heads = range(hg)
        sl = pl.ds(pl.multiple_of(j * blk, blk), blk)
        zs = [lax.dot_general(q_ref[hh], k_ref[hh, sl, :], NT_DIMS, preferred_element_type=F32)
              for hh in heads]
        if diag:
            r = lax.broadcasted_iota(jnp.int32, (tq, blk), 0)
            c = lax.broadcasted_iota(jnp.int32, (tq, blk), 1)
            strict = c < r
        cums = []
        for hh in heads:
            z = zs[hh]
            lg = jnp.minimum(-z, 0.0) - jnp.log(1.0 + jnp.exp(-jnp.abs(z)))
            if diag:
                lg = jnp.where(strict, lg, 0.0)
            hi, lo = _split_hi_lo(lg)
            cums.append(jnp.dot(hi, tri, preferred_element_type=F32)
                        + jnp.dot(lo, tri, preferred_element_type=F32))
        new = []
        for hh in heads:
            carry, decay, acc = states[hh]
            w = jnp.exp(zs[hh] + cums[hh])
            if diag:
                w = jnp.where(strict, w, 0.0)
            pv = jnp.dot(w.astype(BF16), v_ref[hh, sl, :], preferred_element_type=F32)
            carry = carry + cums[hh][:, 0:1]
            new.append((carry, jnp.exp(carry), acc + decay * pv))
        return tuple(new)

    def any_alive(states):
        alive = jnp.max(states[0][1])
        for st in states[1:]:
            alive = jnp.maximum(alive, jnp.max(st[1]))
        return alive > 0.0

    init = (jnp.zeros((tq, 1), F32), jnp.ones((tq, 1), F32), jnp.zeros((tq, LANES), F32))
    states = tiles(i, (init,) * hg, True)

    def cond(c):
        t, alive, _ = c
        return (t < i) & alive

    def body(c):
        t, _, states = c
        states = tiles(i - 1 - t, states, False)
        return t + 1, any_alive(states), states

    _, _, states = lax.while_loop(cond, body, (jnp.int32(0), any_alive(states), states))
    for hh in range(hg):
        o_ref[hh] = states[hh][2].astype(BF16)


def _sb(qkv, tri, B, S):
    T = B * S
    nq = S // MOBA_BLOCK
    H = N_HEADS
    hg = SB_HEADS_PER_STEP
    G = H // hg
    return pl.pallas_call(
        _sb_kernel,
        grid=(B * G, nq),
        in_specs=[pl.BlockSpec((hg, MOBA_BLOCK, LANES), lambda g, i: (3 * G + g % G, (g // G) * nq + i, 0)),
                  pl.BlockSpec((hg, S, LANES), lambda g, i: (4 * G + g % G, g // G, 0)),
                  pl.BlockSpec((hg, S, LANES), lambda g, i: (5 * G + g % G, g // G, 0)),
                  pl.BlockSpec((MOBA_BLOCK, MOBA_BLOCK), lambda g, i: (0, 0))],
        out_specs=pl.BlockSpec((hg, MOBA_BLOCK, LANES), lambda g, i: (g % G, (g // G) * nq + i, 0)),
        out_shape=jax.ShapeDtypeStruct((H, T, LANES), BF16),
        compiler_params=_cparams(("arbitrary", "arbitrary")),
        name="sb_attn",
    )(qkv, qkv, qkv, tri)


def _mix_kernel(x_ref, oa_ref, ob_ref, gate_ref, woa_ref, wob_ref, wmix_ref, g2_ref, wq_ref,
                h_ref, x2_ref, qp_ref):
    D = x_ref.shape[1]

    def out_proj(o_ref, w_ref):
        y = None
        for p in range(N_HEADS // 2):
            pair = jnp.concatenate([o_ref[2 * p], o_ref[2 * p + 1]], axis=1)
            part = jnp.dot(pair, w_ref[p], preferred_element_type=F32)
            y = part if y is None else y + part
        return y

    ya = out_proj(oa_ref, woa_ref)
    yb = out_proj(ob_ref, wob_ref)
    mixed = gate_ref[:, :D].astype(F32) * ya + gate_ref[:, D:].astype(F32) * yb
    h = x_ref[...] + jnp.dot(mixed.astype(BF16), wmix_ref[...], preferred_element_type=F32)
    h_ref[...] = h
    x2 = _rmsnorm(h, g2_ref[...]).astype(BF16)
    x2_ref[...] = x2
    qp_ref[...] = jnp.dot(x2, wq_ref[...], preferred_element_type=F32)


def _mix(x2d, oa, ob, gates, woa, wob, wmix, g2, wq, *, tm):
    T, D = x2d.shape
    nqp = wq.shape[1]
    H = N_HEADS
    const = lambda *shape: pl.BlockSpec(shape, lambda i: (0,) * len(shape))
    return pl.pallas_call(
        _mix_kernel,
        grid=(T // tm,),
        in_specs=[pl.BlockSpec((tm, D), lambda i: (i, 0)),
                  pl.BlockSpec((H, tm, LANES), lambda i: (0, i, 0)),
                  pl.BlockSpec((H, tm, LANES), lambda i: (0, i, 0)),
                  pl.BlockSpec((tm, 2 * D), lambda i: (i, 0)),
                  const(H // 2, 2 * LANES, D), const(H // 2, 2 * LANES, D), const(D, D), const(1, D),
                  const(D, nqp)],
        out_specs=[pl.BlockSpec((tm, D), lambda i: (i, 0)),
                   pl.BlockSpec((tm, D), lambda i: (i, 0)),
                   pl.BlockSpec((tm, nqp), lambda i: (i, 0))],
        out_shape=[jax.ShapeDtypeStruct((T, D), F32),
                   jax.ShapeDtypeStruct((T, D), BF16),
                   jax.ShapeDtypeStruct((T, nqp), F32)],
        compiler_params=_cparams(("arbitrary",)),
        name="mix_out",
    )(x2d, oa, ob, gates, woa, wob, wmix, g2, wq)


_CAND_SLABS = ((0, 16, 0), (1, 8, 16), (2, 5, 24), (3, 4, 32), (4, 3, 40), (5, 2, 48), (6, 2, 56), (7, 2, 64))
_CAND_TAIL_ROW0 = 72
_CAND_ROWS = 80


def _topk_rank(x, k, top_ref, *, tie_break, track_rank=True):
    R, n = x.shape
    row = lax.broadcasted_iota(jnp.int32, (R, n), 0)
    rank = jnp.full((R, n), k, jnp.int32) if track_rank else None
    vals = []
    for r in range(k):
        m = jnp.max(x, axis=0, keepdims=True)
        hit = x == m
        if tie_break:
            hit = row == jnp.min(jnp.where(hit, row, R), axis=0, keepdims=True)
        if track_rank:
            rank = jnp.where(hit, r, rank)
        x = jnp.where(hit, -jnp.inf, x)
        vals.append(m)
        if top_ref is not None:
            top_ref[r:r + 1, :] = m
    taken = (rank < k) if track_rank else (x == -jnp.inf)
    count = jnp.sum(taken.astype(F32), axis=0, keepdims=True)
    return rank, vals, count


def _route_select(s1, s2, write, top1_ref, top2_ref, cand_ref, *, tie_break):
    K = PEER_TOPK
    n = s1.shape[1]
    rank1, _, n1 = _topk_rank(s1, K, top1_ref, tie_break=tie_break, track_rank=tie_break)
    rank2, _, n2 = _topk_rank(s2, K, top2_ref, tie_break=tie_break)
    t1 = top1_ref[...]
    t2 = top2_ref[...]
    row8 = lax.broadcasted_iota(jnp.int32, (8, n), 0)
    for a, nvalid, off in _CAND_SLABS:
        rows = 16 if nvalid == 16 else 8
        slab = t1[a:a + 1, :] + t2[0:rows, :]
        if nvalid < rows:
            slab = jnp.where(row8 < nvalid, slab, -jnp.inf)
        cand_ref[off:off + rows, :] = slab
    cand_ref[_CAND_TAIL_ROW0:_CAND_ROWS, :] = t1[8:16, :] + t2[0:1, :]

    crank, cvals, nc = _topk_rank(cand_ref[...], K, None, tie_break=tie_break)
    sel = (crank < K).astype(F32)
    z = jnp.zeros((1, n), F32)
    for r in range(K):
        z = z + jnp.exp(cvals[r] - cvals[0])
    lr = jnp.zeros(s1.shape, F32)
    for a in range(K):
        if a < 8:
            _, nvalid, off = _CAND_SLABS[a]
            rows = 16 if nvalid == 16 else 8
            cnt = jnp.sum(sel[off:off + rows, :], axis=0, keepdims=True)
        else:
            cnt = sel[_CAND_TAIL_ROW0 + a - 8:_CAND_TAIL_ROW0 + a - 7, :]
        lr = jnp.where((rank1 == a) if tie_break else (s1 == t1[a:a + 1, :]), cnt, lr)
    write(lr, jnp.exp(s1 - t1[0:1, :]) / z, rank2.astype(F32), jnp.exp(s2 - t2[0:1, :]))
    excess = jnp.maximum(jnp.maximum(jnp.abs(n1 - K), jnp.abs(n2 - K)), jnp.abs(nc - K))
    return jnp.max(excess) == 0.0


BF16_SUBLANES = 16
PEER_ROWBLOCKS_PER_DOT = 2


def _bcast_rows_bf16(row, n):
    tile = jnp.broadcast_to(row, (BF16_SUBLANES, row.shape[1])).astype(BF16)
    return jnp.concatenate([tile] * (n // BF16_SUBLANES), axis=0)


def _peer_kernel(qp_ref, sk_ref, x2_ref, h_ref, u_ref, vt_ref, gf_ref, o_ref,
                 acc_ref, cnt_s, e1_s, r2p_s, e2p_s, top1_ref, top2_ref, cand_ref,
                 *, n_i, n_tiles, final_norm):
    i = pl.program_id(0)
    e = pl.program_id(1)
    last = pl.num_programs(1) - 1
    wslot = i % 2
    rslot = 1 - wslot
    NK = PEER_NKEYS
    tt = x2_ref.shape[0]

    def scores():
        q = qp_ref[...]
        half = q.shape[1] // 2
        s1 = lax.dot_general(sk_ref[0, 0], q[:, :half], NT_DIMS, precision=lax.Precision.HIGHEST,
                             preferred_element_type=F32)
        s2 = lax.dot_general(sk_ref[0, 1], q[:, half:], NT_DIMS, precision=lax.Precision.HIGHEST,
                             preferred_element_type=F32)
        return s1, s2

    def write(cnt, e1, rank2, e2):
        cnt_s[wslot, e] = cnt
        e1_s[wslot, e] = e1
        r2p_s[wslot, e] = pltpu.bitcast(rank2.astype(BF16), jnp.uint32)
        e2p_s[wslot, e] = pltpu.bitcast(e2.astype(BF16), jnp.uint32)

    def route(s1, s2, tie_break):
        return _route_select(s1, s2, write, top1_ref, top2_ref, cand_ref, tie_break=tie_break)

    def route_exact_if_tied(s1, s2, tie_free):
        @pl.when(jnp.logical_not(tie_free))
        def _():
            route(s1, s2, True)

    def experts():
        zero = jnp.zeros((NK, LANES), BF16)
        cnts = [[cnt_s[rslot, hh, pl.ds(e * n_i + ii, 1), :] for hh in range(PEER_HEADS)]
                for ii in range(n_i)]
        e1s = [[e1_s[rslot, hh, pl.ds(e * n_i + ii, 1), :] for hh in range(PEER_HEADS)]
               for ii in range(n_i)]
        x2 = x2_ref[...]
        w_rows = []
        for ii in range(n_i):
            if ii % PEER_ROWBLOCKS_PER_DOT == 0:
                r0 = ii * NK
                at = lax.dot_general(u_ref[r0:r0 + PEER_ROWBLOCKS_PER_DOT * NK, :], x2, NT_DIMS,
                                     preferred_element_type=F32)
            ar0 = (ii % PEER_ROWBLOCKS_PER_DOT) * NK
            w_cols = []
            for c0 in range(0, tt, LANES):
                cols = slice(c0, c0 + LANES)
                a = at[ar0:ar0 + NK, cols]
                ga = (a + a * lax.erf(a)).astype(BF16)
                coef = None
                for hh in range(PEER_HEADS):
                    cnt = _bcast_rows_bf16(cnts[ii][hh][:, cols], NK)
                    e1 = _bcast_rows_bf16(e1s[ii][hh][:, cols], NK)
                    r2 = pltpu.bitcast(r2p_s[rslot, hh, :, cols], BF16)
                    e2 = pltpu.bitcast(e2p_s[rslot, hh, :, cols], BF16)
                    term = jnp.where(r2 < cnt, e2, zero) * e1
                    coef = term if coef is None else coef + term
                w_cols.append(coef * ga)
            w_rows.append(jnp.concatenate(w_cols, axis=1))
        w = jnp.concatenate(w_rows, axis=0)
        acc_ref[...] += lax.dot_general(vt_ref[...], w, (((0,), (0,)), ((), ())),
                                        preferred_element_type=F32)

    def finish_tile():
        @pl.when(e == last)
        def _():
            y = h_ref[...] + acc_ref[...].T
            o_ref[...] = _rmsnorm(y, gf_ref[...]) if final_norm else y
            acc_ref[...] = jnp.zeros_like(acc_ref)

    @pl.when(i == 0)
    def _():
        @pl.when(e == 0)
        def _():
            acc_ref[...] = jnp.zeros_like(acc_ref)

        s1, s2 = scores()
        route_exact_if_tied(s1, s2, route(s1, s2, False))

    @pl.when((i > 0) & (i < n_tiles))
    def _():
        s1, s2 = scores()
        tie_free = route(s1, s2, False)
        experts()
        route_exact_if_tied(s1, s2, tie_free)
        finish_tile()

    @pl.when(i == n_tiles)
    def _():
        experts()
        finish_tile()


def _peer(qp, sub_keys, x2, h, u, vt, gf, *, tt, eb, final_norm):
    T, D = h.shape
    NE = u.shape[0]
    PH, NK = PEER_HEADS, PEER_NKEYS
    n_tiles = T // tt
    assert NE // eb == PH
    qd = qp.shape[1] // PH
    prev = lambda i: jnp.maximum(i - 1, 0)
    blk = lambda i, e: jnp.where(i == 0, 0, e)
    return pl.pallas_call(
        functools.partial(_peer_kernel, n_i=eb // NK, n_tiles=n_tiles, final_norm=final_norm),
        grid=(n_tiles + 1, PH),
        in_specs=[pl.BlockSpec((tt, qd), lambda i, e: (jnp.minimum(i, n_tiles - 1), e)),
                  pl.BlockSpec((1, 2, NK, qd // 2), lambda i, e: (e, 0, 0, 0)),
                  pl.BlockSpec((tt, D), lambda i, e: (prev(i), 0)),
                  pl.BlockSpec((tt, D), lambda i, e: (prev(i), 0)),
                  pl.BlockSpec((eb, D), lambda i, e: (blk(i, e), 0)),
                  pl.BlockSpec((eb, D), lambda i, e: (blk(i, e), 0)),
                  pl.BlockSpec((1, D), lambda i, e: (0, 0))],
        out_specs=pl.BlockSpec((tt, D), lambda i, e: (prev(i), 0)),
        out_shape=jax.ShapeDtypeStruct((T, D), F32),
        scratch_shapes=[pltpu.VMEM((D, tt), F32),
                        pltpu.VMEM((2, PH, NK, tt), F32), pltpu.VMEM((2, PH, NK, tt), F32),
                        pltpu.VMEM((2, PH, NK // 2, tt), jnp.uint32),
                        pltpu.VMEM((2, PH, NK // 2, tt), jnp.uint32),
                        pltpu.VMEM((PEER_TOPK, tt), F32), pltpu.VMEM((PEER_TOPK, tt), F32),
                        pltpu.VMEM((_CAND_ROWS, tt), F32)],
        compiler_params=_cparams(("arbitrary", "arbitrary")),
        name="peer_route_experts",
    )(qp, sub_keys, x2, h, u, vt, gf)


def _pad_heads_cols(w):
    D, n = w.shape
    w = w.reshape(D, n // HEAD_DIM, HEAD_DIM)
    w = jnp.pad(w, ((0, 0), (0, 0), (0, LANES - HEAD_DIM)))
    return w.reshape(D, (n // HEAD_DIM) * LANES)


def _pad_heads_rows(w):
    n, D = w.shape
    w = w.reshape(n // HEAD_DIM, HEAD_DIM, D)
    w = jnp.pad(w, ((0, 0), (0, LANES - HEAD_DIM), (0, 0)))
    return w.reshape(n // HEAD_DIM // 2, 2 * LANES, D)


def kernel(x, norm1_g, w_in, w_out_moba, w_out_sb, w_mix_out, norm2_g, peer_w_q, peer_sub_keys, peer_u,
           peer_v, final_norm_g):
    B, S, D = x.shape
    T = B * S
    depth = w_in.shape[0]
    n_qkv = 6 * N_HEADS * HEAD_DIM
    tm = min(1024, T)
    slopes = jnp.exp2(-8.0 * jnp.arange(1, N_HEADS + 1, dtype=F32) / N_HEADS)
    tri = jnp.tril(jnp.ones((MOBA_BLOCK, MOBA_BLOCK), F32)).astype(BF16)

    h = x.reshape(T, D)
    out = h
    for l in range(depth):
        w_heads = _pad_heads_cols(w_in[l][:, :n_qkv]).astype(BF16)
        w_gates = w_in[l][:, n_qkv:].astype(BF16)
        g1 = norm1_g[l].reshape(1, D)
        qkv = _proj(h, g1, w_heads, per_head=True, tm=tm)
        gates = _proj(h, g1, w_gates, per_head=False, tm=tm)
        oa = _moba(qkv, slopes, B, S)
        ob = _sb(qkv, tri, B, S)
        h, x2, qp = _mix(h, oa, ob, gates,
                         _pad_heads_rows(w_out_moba[l]).astype(BF16),
                         _pad_heads_rows(w_out_sb[l]).astype(BF16),
                         w_mix_out[l].astype(BF16), norm2_g[l].reshape(1, D),
                         peer_w_q[l].astype(BF16), tm=min(512, T))
        out = _peer(qp, peer_sub_keys[l], x2, h, (peer_u[l] * INV_SQRT2).astype(BF16),
                    (peer_v[l] * INV_SQRT2).astype(BF16),
                    final_norm_g.reshape(1, D), tt=min(512, T), eb=2048, final_norm=(l == depth - 1))
        h = out
    return out.reshape(B, S, D)
```

```python
import functools

import jax
import jax.numpy as jnp
from jax import lax
from jax.experimental import pallas as pl
from jax.experimental.pallas import tpu as pltpu

F32 = jnp.float32
BF16 = jnp.bfloat16

HEAD_DIM = 64
N_HEADS = 8
LANES = 128
MOBA_BLOCK = 256
MOBA_TOPK = 3
PEER_HEADS = 8
PEER_NKEYS = 128
PEER_TOPK = 16
RMS_EPS = 1e-6
NEG_BIG = -1e30
INV_SQRT2 = 0.7071067811865476
VMEM_LIMIT = 56 * 1024 * 1024

LANE_BLK0 = 64
MAX_KV_BLOCKS = 32
LANE_QPOS = 96
LANE_KPOS = 99
MOBA_HEADS_PER_STEP = 4
MOBA_KV_CHUNK = 8
SB_HEADS_PER_STEP = 8

PROJ_COLS = N_HEADS * LANES

NT_DIMS = (((1,), (1,)), ((), ()))


def _cparams(sem):
    return pltpu.CompilerParams(dimension_semantics=sem, vmem_limit_bytes=VMEM_LIMIT)


def _rmsnorm(x, g):
    return x * lax.rsqrt(jnp.mean(x * x, axis=-1, keepdims=True) + RMS_EPS) * g


def _split_hi_lo(x):
    hi = x.astype(BF16)
    lo = (x - hi.astype(F32)).astype(BF16)
    return hi, lo


def _split3(x):
    a = x.astype(BF16).astype(F32)
    r = x - a
    b = r.astype(BF16).astype(F32)
    c = (r - b).astype(BF16).astype(F32)
    return a, b, c


def _proj_kernel(x_ref, g_ref, w_ref, o_ref, xn_ref, *, per_head):
    j = pl.program_id(1)

    @pl.when(j == 0)
    def _():
        xn_ref[...] = _rmsnorm(x_ref[...], g_ref[...]).astype(BF16)

    res = jnp.dot(xn_ref[...], w_ref[...], preferred_element_type=F32)
    if per_head:
        res = res * jnp.where((j == 0) | (j == 3), HEAD_DIM ** -0.5, 1.0)
        for h in range(N_HEADS):
            o_ref[h] = res[:, h * LANES:(h + 1) * LANES].astype(BF16)
    else:
        o_ref[...] = jax.nn.sigmoid(res).astype(BF16)


def _proj(x2d, g, w, *, per_head, tm):
    T, D = x2d.shape
    ncol = w.shape[1] // PROJ_COLS
    if per_head:
        out_shape = jax.ShapeDtypeStruct((ncol * N_HEADS, T, LANES), BF16)
        out_spec = pl.BlockSpec((N_HEADS, tm, LANES), lambda i, j: (j, i, 0))
    else:
        out_shape = jax.ShapeDtypeStruct((T, w.shape[1]), BF16)
        out_spec = pl.BlockSpec((tm, PROJ_COLS), lambda i, j: (i, j))
    return pl.pallas_call(
        functools.partial(_proj_kernel, per_head=per_head),
        grid=(T // tm, ncol),
        in_specs=[pl.BlockSpec((tm, D), lambda i, j: (i, 0)),
                  pl.BlockSpec((1, D), lambda i, j: (0, 0)),
                  pl.BlockSpec((D, PROJ_COLS), lambda i, j: (0, j))],
        out_specs=out_spec,
        out_shape=out_shape,
        scratch_shapes=[pltpu.VMEM((tm, D), BF16)],
        compiler_params=_cparams(("arbitrary", "arbitrary")),
        name="proj_heads" if per_head else "proj_gates",
    )(x2d, g, w)


def _moba_kernel(slope_ref, q_ref, k_ref, v_ref, o_ref, kaug_ref, vaug_ref, kmean_ref, *, nb, groups):
    g = pl.program_id(0)
    i = pl.program_id(1)
    hg, tq, _ = q_ref.shape
    blk = MOBA_BLOCK
    head0 = (g % groups) * hg

    @pl.when(i == 0)
    def _():
        kmean_ref[...] = jnp.zeros_like(kmean_ref)
        row = lax.broadcasted_iota(jnp.int32, (blk, LANES), 0)
        lane = lax.broadcasted_iota(jnp.int32, (blk, LANES), 1)
        ones_q = jnp.where((lane >= LANE_QPOS) & (lane < LANE_QPOS + 3), 1.0, 0.0)
        for hh in range(hg):
            slope = slope_ref[head0 + hh]

            def fill(n, c, hh=hh, slope=slope):
                sl = pl.ds(pl.multiple_of(n * blk, blk), blk)
                kf = k_ref[hh, sl, :].astype(F32)
                kmean_ref[hh, pl.ds(n, 1), :] = jnp.mean(kf, axis=0, keepdims=True)
                pa, pb, pc = _split3(slope * (n * blk + row).astype(F32))
                feat = jnp.where(lane == LANE_BLK0 + n, 1.0, ones_q)
                feat = jnp.where(lane == LANE_KPOS, pa, feat)
                feat = jnp.where(lane == LANE_KPOS + 1, pb, feat)
                feat = jnp.where(lane == LANE_KPOS + 2, pc, feat)
                kaug_ref[hh, sl, :] = jnp.where(lane < HEAD_DIM, kf, feat).astype(BF16)
                vf = v_ref[hh, sl, :].astype(F32)
                vaug_ref[hh, sl, :] = jnp.where(lane == HEAD_DIM, 1.0, vf).astype(BF16)
                return c

            lax.fori_loop(0, nb, fill, 0)

    heads = range(hg)
    lane = lax.broadcasted_iota(jnp.int32, (tq, LANES), 1)
    is_blk_lane = (lane >= LANE_BLK0) & (lane < LANE_BLK0 + MAX_KV_BLOCKS)
    rowb = lax.broadcasted_iota(jnp.int32, (MAX_KV_BLOCKS, tq), 0)
    tpos = (i * blk + lax.broadcasted_iota(jnp.int32, (tq, LANES), 0)).astype(F32)
    own_sl = pl.ds(pl.multiple_of(i * blk, blk), blk)
    r = lax.broadcasted_iota(jnp.int32, (tq, blk), 0)
    c = lax.broadcasted_iota(jnp.int32, (tq, blk), 1)
    causal = c <= r

    qfs = [q_ref[hh].astype(F32) for hh in heads]
    gate = [lax.dot_general(kmean_ref[hh], qfs[hh], NT_DIMS, precision=lax.Precision.HIGHEST,
                            preferred_element_type=F32) for hh in heads]
    qowns, own_scores = [], []
    for hh in heads:
        pa, pb, pc = _split3(-slope_ref[head0 + hh] * tpos)
        feat = jnp.where(lane == LANE_QPOS, pa, jnp.where(lane == LANE_QPOS + 1, pb,
                         jnp.where(lane == LANE_QPOS + 2, pc,
                                   jnp.where((lane >= LANE_KPOS) & (lane < LANE_KPOS + 3), 1.0, 0.0))))
        qown = jnp.where(lane < HEAD_DIM, qfs[hh], feat).astype(BF16)
        qowns.append(qown)
        own_scores.append(lax.dot_general(qown, kaug_ref[hh, own_sl, :], NT_DIMS,
                                          preferred_element_type=F32))
    qaugs = []
    for hh in heads:
        bsm = jnp.where(rowb < i, gate[hh], -jnp.inf)
        sel = jnp.zeros(bsm.shape, jnp.bool_)
        for _ in range(MOBA_TOPK):
            m = jnp.max(bsm, axis=0, keepdims=True)
            p = jnp.min(jnp.where(bsm == m, rowb, MAX_KV_BLOCKS), axis=0, keepdims=True)
            hit = rowb == p
            sel = sel | (hit & (m > -jnp.inf))
            bsm = jnp.where(hit, -jnp.inf, bsm)
        bias_t = jnp.where(sel, 0.0, NEG_BIG)
        bias = jnp.concatenate([jnp.zeros((LANE_BLK0, tq), F32), bias_t,
                                jnp.zeros((LANES - LANE_QPOS, tq), F32)], axis=0).T
        qaugs.append(jnp.where(is_blk_lane, bias.astype(BF16), qowns[hh]))
    states = []
    for hh in heads:
        s0 = jnp.where(causal, own_scores[hh], NEG_BIG)
        m0 = jnp.max(s0, axis=1, keepdims=True)
        p0 = jnp.exp(s0 - m0)
        acc0 = jnp.dot(p0.astype(BF16), vaug_ref[hh, own_sl, :], preferred_element_type=F32)
        states.append((m0, acc0))

    span = MOBA_KV_CHUNK * blk

    def body(cidx, states):
        sl = pl.ds(pl.multiple_of(cidx * span, span), span)
        scores = [lax.dot_general(qaugs[hh], kaug_ref[hh, sl, :], NT_DIMS, preferred_element_type=F32)
                  for hh in range(hg)]
        new = []
        for hh in range(hg):
            m, acc = states[hh]
            s = scores[hh]
            mn = jnp.maximum(m, jnp.max(s, axis=1, keepdims=True))
            alpha = jnp.exp(m - mn)
            p = jnp.exp(s - mn)
            acc = alpha * acc + jnp.dot(p.astype(BF16), vaug_ref[hh, sl, :], preferred_element_type=F32)
            new.append((mn, acc))
        return tuple(new)

    n_chunks = (i + MOBA_KV_CHUNK - 1) // MOBA_KV_CHUNK
    states = lax.fori_loop(0, n_chunks, body, tuple(states))
    for hh in range(hg):
        acc = states[hh][1]
        denom = acc[:, HEAD_DIM:HEAD_DIM + 1]
        o_ref[hh] = jnp.where(lane < HEAD_DIM, acc / denom, 0.0).astype(BF16)


def _moba(qkv, slopes, B, S):
    T = B * S
    nq = S // MOBA_BLOCK
    nb = S // MOBA_BLOCK
    assert nb <= MAX_KV_BLOCKS and nb % MOBA_KV_CHUNK == 0
    H = N_HEADS
    hg = MOBA_HEADS_PER_STEP
    G = H // hg
    return pl.pallas_call(
        functools.partial(_moba_kernel, nb=nb, groups=G),
        grid=(B * G, nq),
        in_specs=[pl.BlockSpec(memory_space=pltpu.SMEM),
                  pl.BlockSpec((hg, MOBA_BLOCK, LANES), lambda g, i: (g % G, (g // G) * nq + i, 0)),
                  pl.BlockSpec((hg, S, LANES), lambda g, i: (G + g % G, g // G, 0),
                               pipeline_mode=pl.Buffered(1)),
                  pl.BlockSpec((hg, S, LANES), lambda g, i: (2 * G + g % G, g // G, 0),
                               pipeline_mode=pl.Buffered(1))],
        out_specs=pl.BlockSpec((hg, MOBA_BLOCK, LANES), lambda g, i: (g % G, (g // G) * nq + i, 0)),
        out_shape=jax.ShapeDtypeStruct((H, T, LANES), BF16),
        scratch_shapes=[pltpu.VMEM((hg, S, LANES), BF16), pltpu.VMEM((hg, S, LANES), BF16),
                        pltpu.VMEM((hg, MAX_KV_BLOCKS, LANES), F32)],
        compiler_params=_cparams(("arbitrary", "arbitrary")),
        name="moba_attn",
    )(slopes, qkv, qkv, qkv)


def _sb_kernel(q_ref, k_ref, v_ref, tri_ref, o_ref):
    i = pl.program_id(1)
    blk = MOBA_BLOCK
    hg, tq, _ = q_ref.shape
    tri = tri_ref[...]

    def tiles(j, states, diag):
        heads = range(hg)
        sl = pl.ds(pl.multiple_of(j * blk, blk), blk)
        zs = [lax.dot_general(q_ref[hh], k_ref[hh, sl, :], NT_DIMS, preferred_element_type=F32)
              for hh in heads]
        if diag:
            r = lax.broadcasted_iota(jnp.int32, (tq, blk), 0)
            c = lax.broadcasted_iota(jnp.int32, (tq, blk), 1)
            strict = c < r
        cums = []
        for hh in heads:
            z = zs[hh]
            lg = jnp.minimum(-z, 0.0) - jnp.log(1.0 + jnp.exp(-jnp.abs(z)))
            if diag:
                lg = jnp.where(strict, lg, 0.0)
            hi, lo = _split_hi_lo(lg)
            cums.append(jnp.dot(hi, tri, preferred_element_type=F32)
                        + jnp.dot(lo, tri, preferred_element_type=F32))
        new = []
        for hh in heads:
            carry, decay, acc = states[hh]
            w = jnp.exp(zs[hh] + cums[hh])
            if diag:
                w = jnp.where(strict, w, 0.0)
            pv = jnp.dot(w.astype(BF16), v_ref[hh, sl, :], preferred_element_type=F32)
            carry = carry + cums[hh][:, 0:1]
            new.append((carry, jnp.exp(carry), acc + decay * pv))
        return tuple(new)

    def any_alive(states):
        alive = jnp.max(states[0][1])
        for st in states[1:]:
            alive = jnp.maximum(alive, jnp.max(st[1]))
        return alive > 0.0

    init = (jnp.zeros((tq, 1), F32), jnp.ones((tq, 1), F32), jnp.zeros((tq, LANES), F32))
    states = tiles(i, (init,) * hg, True)

    def cond(c):
        t, alive, _ = c
        return (t < i) & alive

    def body(c):
        t, _, states = c
        states = tiles(i - 1 - t, states, False)
        return t + 1, any_alive(states), states

    _, _, states = lax.while_loop(cond, body, (jnp.int32(0), any_alive(states), states))
    for hh in range(hg):
        o_ref[hh] = states[hh][2].astype(BF16)


def _sb(qkv, tri, B, S):
    T = B * S
    nq = S // MOBA_BLOCK
    H = N_HEADS
    hg = SB_HEADS_PER_STEP
    G = H // hg
    return pl.pallas_call(
        _sb_kernel,
        grid=(B * G, nq),
        in_specs=[pl.BlockSpec((hg, MOBA_BLOCK, LANES), lambda g, i: (3 * G + g % G, (g // G) * nq + i, 0)),
                  pl.BlockSpec((hg, S, LANES), lambda g, i: (4 * G + g % G, g // G, 0),
                               pipeline_mode=pl.Buffered(1)),
                  pl.BlockSpec((hg, S, LANES), lambda g, i: (5 * G + g % G, g // G, 0),
                               pipeline_mode=pl.Buffered(1)),
                  pl.BlockSpec((MOBA_BLOCK, MOBA_BLOCK), lambda g, i: (0, 0))],
        out_specs=pl.BlockSpec((hg, MOBA_BLOCK, LANES), lambda g, i: (g % G, (g // G) * nq + i, 0)),
        out_shape=jax.ShapeDtypeStruct((H, T, LANES), BF16),
        compiler_params=_cparams(("arbitrary", "arbitrary")),
        name="sb_attn",
    )(qkv, qkv, qkv, tri)


def _mix_kernel(x_ref, oa_ref, ob_ref, gate_ref, woa_ref, wob_ref, wmix_ref, g2_ref, wq_ref,
                h_ref, x2_ref, qp_ref):
    D = x_ref.shape[1]

    def out_proj(o_ref, w_ref):
        y = None
        for p in range(N_HEADS // 2):
            pair = jnp.concatenate([o_ref[2 * p], o_ref[2 * p + 1]], axis=1)
            part = jnp.dot(pair, w_ref[p], preferred_element_type=F32)
            y = part if y is None else y + part
        return y

    ya = out_proj(oa_ref, woa_ref)
    yb = out_proj(ob_ref, wob_ref)
    mixed = gate_ref[:, :D].astype(F32) * ya + gate_ref[:, D:].astype(F32) * yb
    h = x_ref[...] + jnp.dot(mixed.astype(BF16), wmix_ref[...], preferred_element_type=F32)
    h_ref[...] = h
    x2 = _rmsnorm(h, g2_ref[...]).astype(BF16)
    x2_ref[...] = x2
    qp_ref[...] = jnp.dot(x2, wq_ref[...], preferred_element_type=F32)


def _mix(x2d, oa, ob, gates, woa, wob, wmix, g2, wq, *, tm):
    T, D = x2d.shape
    nqp = wq.shape[1]
    H = N_HEADS
    const = lambda *shape: pl.BlockSpec(shape, lambda i: (0,) * len(shape))
    return pl.pallas_call(
        _mix_kernel,
        grid=(T // tm,),
        in_specs=[pl.BlockSpec((tm, D), lambda i: (i, 0)),
                  pl.BlockSpec((H, tm, LANES), lambda i: (0, i, 0)),
                  pl.BlockSpec((H, tm, LANES), lambda i: (0, i, 0)),
                  pl.BlockSpec((tm, 2 * D), lambda i: (i, 0)),
                  const(H // 2, 2 * LANES, D), const(H // 2, 2 * LANES, D), const(D, D), const(1, D),
                  const(D, nqp)],
        out_specs=[pl.BlockSpec((tm, D), lambda i: (i, 0)),
                   pl.BlockSpec((tm, D), lambda i: (i, 0)),
                   pl.BlockSpec((tm, nqp), lambda i: (i, 0))],
        out_shape=[jax.ShapeDtypeStruct((T, D), F32),
                   jax.ShapeDtypeStruct((T, D), BF16),
                   jax.ShapeDtypeStruct((T, nqp), F32)],
        compiler_params=_cparams(("arbitrary",)),
        name="mix_out",
    )(x2d, oa, ob, gates, woa, wob, wmix, g2, wq)


_CAND_SLABS = ((0, 16, 0), (1, 8, 16), (2, 5, 24), (3, 4, 32), (4, 3, 40), (5, 2, 48), (6, 2, 56), (7, 2, 64))
_CAND_TAIL_ROW0 = 72
_CAND_ROWS = 80


def _topk_rank(x, k, top_ref, *, tie_break, track_rank=True):
    R, n = x.shape
    row = lax.broadcasted_iota(jnp.int32, (R, n), 0)
    rank = jnp.full((R, n), k, jnp.int32) if track_rank else None
    vals = []
    for r in range(k):
        m = jnp.max(x, axis=0, keepdims=True)
        hit = x == m
        if tie_break:
            hit = row == jnp.min(jnp.where(hit, row, R), axis=0, keepdims=True)
        if track_rank:
            rank = jnp.where(hit, r, rank)
        x = jnp.where(hit, -jnp.inf, x)
        vals.append(m)
        if top_ref is not None:
            top_ref[r:r + 1, :] = m
    taken = (rank < k) if track_rank else (x == -jnp.inf)
    count = jnp.sum(taken.astype(F32), axis=0, keepdims=True)
    return rank, vals, count


def _route_select(s1, s2, write, top1_ref, top2_ref, cand_ref, *, tie_break):
    K = PEER_TOPK
    n = s1.shape[1]
    rank1, _, n1 = _topk_rank(s1, K, top1_ref, tie_break=tie_break, track_rank=tie_break)
    rank2, _, n2 = _topk_rank(s2, K, top2_ref, tie_break=tie_break)
    t1 = top1_ref[...]
    t2 = top2_ref[...]
    row8 = lax.broadcasted_iota(jnp.int32, (8, n), 0)
    for a, nvalid, off in _CAND_SLABS:
        rows = 16 if nvalid == 16 else 8
        slab = t1[a:a + 1, :] + t2[0:rows, :]
        if nvalid < rows:
            slab = jnp.where(row8 < nvalid, slab, -jnp.inf)
        cand_ref[off:off + rows, :] = slab
    cand_ref[_CAND_TAIL_ROW0:_CAND_ROWS, :] = t1[8:16, :] + t2[0:1, :]

    crank, cvals, nc = _topk_rank(cand_ref[...], K, None, tie_break=tie_break)
    sel = (crank < K).astype(F32)
    z = jnp.zeros((1, n), F32)
    for r in range(K):
        z = z + jnp.exp(cvals[r] - cvals[0])
    lr = jnp.zeros(s1.shape, F32)
    for a in range(K):
        if a < 8:
            _, nvalid, off = _CAND_SLABS[a]
            rows = 16 if nvalid == 16 else 8
            cnt = jnp.sum(sel[off:off + rows, :], axis=0, keepdims=True)
        else:
            cnt = sel[_CAND_TAIL_ROW0 + a - 8:_CAND_TAIL_ROW0 + a - 7, :]
        lr = jnp.where((rank1 == a) if tie_break else (s1 == t1[a:a + 1, :]), cnt, lr)
    write(lr, jnp.exp(s1 - t1[0:1, :]) / z, rank2.astype(F32), jnp.exp(s2 - t2[0:1, :]))
    excess = jnp.maximum(jnp.maximum(jnp.abs(n1 - K), jnp.abs(n2 - K)), jnp.abs(nc - K))
    return jnp.max(excess) == 0.0


BF16_SUBLANES = 16
PEER_ROWBLOCKS_PER_DOT = 2


def _bcast_rows_bf16(row, n):
    tile = jnp.broadcast_to(row, (BF16_SUBLANES, row.shape[1])).astype(BF16)
    return jnp.concatenate([tile] * (n // BF16_SUBLANES), axis=0)


def _peer_kernel(qp_ref, sk_ref, x2_ref, h_ref, u_ref, v_ref, gf_ref, o_ref,
                 acc_ref, cnt_s, e1_s, r2p_s, e2p_s, top1_ref, top2_ref, cand_ref,
                 *, n_i, n_tiles, final_norm):
    i = pl.program_id(0)
    e = pl.program_id(1)
    last = pl.num_programs(1) - 1
    wslot = i % 2
    rslot = 1 - wslot
    NK = PEER_NKEYS
    tt = x2_ref.shape[0]

    def scores():
        q = qp_ref[...]
        half = q.shape[1] // 2
        s1 = lax.dot_general(sk_ref[0, 0], q[:, :half], NT_DIMS, precision=lax.Precision.HIGHEST,
                             preferred_element_type=F32)
        s2 = lax.dot_general(sk_ref[0, 1], q[:, half:], NT_DIMS, precision=lax.Precision.HIGHEST,
                             preferred_element_type=F32)
        return s1, s2

    def write(cnt, e1, rank2, e2):
        cnt_s[wslot, e] = cnt
        e1_s[wslot, e] = e1
        r2p_s[wslot, e] = pltpu.bitcast(rank2.astype(BF16), jnp.uint32)
        e2p_s[wslot, e] = pltpu.bitcast(e2.astype(BF16), jnp.uint32)

    def route(s1, s2, tie_break):
        return _route_select(s1, s2, write, top1_ref, top2_ref, cand_ref, tie_break=tie_break)

    def route_exact_if_tied(s1, s2, tie_free):
        @pl.when(jnp.logical_not(tie_free))
        def _():
            route(s1, s2, True)

    def experts():
        zero = jnp.zeros((NK, LANES), BF16)
        cnts = [[cnt_s[rslot, hh, pl.ds(e * n_i + ii, 1), :] for hh in range(PEER_HEADS)]
                for ii in range(n_i)]
        e1s = [[e1_s[rslot, hh, pl.ds(e * n_i + ii, 1), :] for hh in range(PEER_HEADS)]
               for ii in range(n_i)]
        x2 = x2_ref[...]
        w_rows = []
        for ii in range(n_i):
            if ii % PEER_ROWBLOCKS_PER_DOT == 0:
                r0 = ii * NK
                at = lax.dot_general(u_ref[r0:r0 + PEER_ROWBLOCKS_PER_DOT * NK, :], x2, NT_DIMS,
                                     preferred_element_type=F32)
            ar0 = (ii % PEER_ROWBLOCKS_PER_DOT) * NK
            w_cols = []
            for c0 in range(0, tt, LANES):
                cols = slice(c0, c0 + LANES)
                a = at[ar0:ar0 + NK, cols]
                ga = (a + a * lax.erf(a)).astype(BF16)
                coef = None
                for hh in range(PEER_HEADS):
                    cnt = _bcast_rows_bf16(cnts[ii][hh][:, cols], NK)
                    e1 = _bcast_rows_bf16(e1s[ii][hh][:, cols], NK)
                    r2 = pltpu.bitcast(r2p_s[rslot, hh, :, cols], BF16)
                    e2 = pltpu.bitcast(e2p_s[rslot, hh, :, cols], BF16)
                    term = jnp.where(r2 < cnt, e2, zero) * e1
                    coef = term if coef is None else coef + term
                w_cols.append(coef * ga)
            w_rows.append(jnp.concatenate(w_cols, axis=1))
        w = jnp.concatenate(w_rows, axis=0)
        acc_ref[...] += lax.dot_general(v_ref[...], w, (((0,), (0,)), ((), ())),
                                        preferred_element_type=F32)

    def finish_tile():
        @pl.when(e == last)
        def _():
            y = h_ref[...] + acc_ref[...].T
            o_ref[...] = _rmsnorm(y, gf_ref[...]) if final_norm else y
            acc_ref[...] = jnp.zeros_like(acc_ref)

    @pl.when(i == 0)
    def _():
        @pl.when(e == 0)
        def _():
            acc_ref[...] = jnp.zeros_like(acc_ref)

        s1, s2 = scores()
        route_exact_if_tied(s1, s2, route(s1, s2, False))

    @pl.when((i > 0) & (i < n_tiles))
    def _():
        s1, s2 = scores()
        tie_free = route(s1, s2, False)
        experts()
        route_exact_if_tied(s1, s2, tie_free)
        finish_tile()

    @pl.when(i == n_tiles)
    def _():
        experts()
        finish_tile()


def _peer(qp, sub_keys, x2, h, u, v, gf, *, tt, eb, final_norm):
    T, D = h.shape
    NE = u.shape[0]
    PH, NK = PEER_HEADS, PEER_NKEYS
    n_tiles = T // tt
    assert NE // eb == PH
    qd = qp.shape[1] // PH
    prev = lambda i: jnp.maximum(i - 1, 0)
    blk = lambda i, e: jnp.where(i == 0, 0, e)
    return pl.pallas_call(
        functools.partial(_peer_kernel, n_i=eb // NK, n_tiles=n_tiles, final_norm=final_norm),
        grid=(n_tiles + 1, PH),
        in_specs=[pl.BlockSpec((tt, qd), lambda i, e: (jnp.minimum(i, n_tiles - 1), e)),
                  pl.BlockSpec((1, 2, NK, qd // 2), lambda i, e: (e, 0, 0, 0)),
                  pl.BlockSpec((tt, D), lambda i, e: (prev(i), 0)),
                  pl.BlockSpec((tt, D), lambda i, e: (prev(i), 0)),
                  pl.BlockSpec((eb, D), lambda i, e: (blk(i, e), 0)),
                  pl.BlockSpec((eb, D), lambda i, e: (blk(i, e), 0)),
                  pl.BlockSpec((1, D), lambda i, e: (0, 0))],
        out_specs=pl.BlockSpec((tt, D), lambda i, e: (prev(i), 0)),
        out_shape=jax.ShapeDtypeStruct((T, D), F32),
        scratch_shapes=[pltpu.VMEM((D, tt), F32),
                        pltpu.VMEM((2, PH, NK, tt), F32), pltpu.VMEM((2, PH, NK, tt), F32),
                        pltpu.VMEM((2, PH, NK // 2, tt), jnp.uint32),
                        pltpu.VMEM((2, PH, NK // 2, tt), jnp.uint32),
                        pltpu.VMEM((PEER_TOPK, tt), F32), pltpu.VMEM((PEER_TOPK, tt), F32),
                        pltpu.VMEM((_CAND_ROWS, tt), F32)],
        compiler_params=_cparams(("arbitrary", "arbitrary")),
        name="peer_route_experts",
    )(qp, sub_keys, x2, h, u, v, gf)


def _pad_heads_cols(w):
    D, n = w.shape
    w = w.reshape(D, n // HEAD_DIM, HEAD_DIM)
    w = jnp.pad(w, ((0, 0), (0, 0), (0, LANES - HEAD_DIM)))
    return w.reshape(D, (n // HEAD_DIM) * LANES)


def _pad_heads_rows(w):
    n, D = w.shape
    w = w.reshape(n // HEAD_DIM, HEAD_DIM, D)
    w = jnp.pad(w, ((0, 0), (0, LANES - HEAD_DIM), (0, 0)))
    return w.reshape(n // HEAD_DIM // 2, 2 * LANES, D)


def kernel(x, norm1_g, w_in, w_out_moba, w_out_sb, w_mix_out, norm2_g, peer_w_q, peer_sub_keys, peer_u,
           peer_v, final_norm_g):
    B, S, D = x.shape
    T = B * S
    depth = w_in.shape[0]
    n_qkv = 6 * N_HEADS * HEAD_DIM
    tm = min(1024, T)
    slopes = jnp.exp2(-8.0 * jnp.arange(1, N_HEADS + 1, dtype=F32) / N_HEADS)
    tri = jnp.tril(jnp.ones((MOBA_BLOCK, MOBA_BLOCK), F32)).astype(BF16)

    h = x.reshape(T, D)
    out = h
    for l in range(depth):
        w_heads = _pad_heads_cols(w_in[l][:, :n_qkv]).astype(BF16)
        w_gates = w_in[l][:, n_qkv:].astype(BF16)
        g1 = norm1_g[l].reshape(1, D)
        qkv = _proj(h, g1, w_heads, per_head=True, tm=tm)
        gates = _proj(h, g1, w_gates, per_head=False, tm=tm)
        oa = _moba(qkv, slopes, B, S)
        ob = _sb(qkv, tri, B, S)
        h, x2, qp = _mix(h, oa, ob, gates,
                         _pad_heads_rows(w_out_moba[l]).astype(BF16),
                         _pad_heads_rows(w_out_sb[l]).astype(BF16),
                         w_mix_out[l].astype(BF16), norm2_g[l].reshape(1, D),
                         peer_w_q[l].astype(BF16), tm=min(512, T))
        out = _peer(qp, peer_sub_keys[l], x2, h, (peer_u[l] * INV_SQRT2).astype(BF16),
                    (peer_v[l] * INV_SQRT2).astype(BF16),
                    final_norm_g.reshape(1, D), tt=min(512, T), eb=2048, final_norm=(l == depth - 1))
        h = out
    return out.reshape(B, S, D)
```

```python
import functools

import jax
import jax.numpy as jnp
from jax import lax
from jax.experimental import pallas as pl
from jax.experimental.pallas import tpu as pltpu

F32 = jnp.float32
BF16 = jnp.bfloat16

HEAD_DIM = 64
N_HEADS = 8
LANES = 128
MOBA_BLOCK = 256
MOBA_TOPK = 3
PEER_HEADS = 8
PEER_NKEYS = 128
PEER_TOPK = 16
RMS_EPS = 1e-6
NEG_BIG = -1e30
INV_SQRT2 = 0.7071067811865476
VMEM_LIMIT = 56 * 1024 * 1024

LANE_BLK0 = 64
MAX_KV_BLOCKS = 32
LANE_QPOS = 96
LANE_KPOS = 99
MOBA_HEADS_PER_STEP = 4
MOBA_KV_CHUNK = 8
SB_HEADS_PER_STEP = 8

PROJ_COLS = N_HEADS * LANES

NT_DIMS = (((1,), (1,)), ((), ()))


def _cparams(sem):
    return pltpu.CompilerParams(dimension_semantics=sem, vmem_limit_bytes=VMEM_LIMIT)


def _rmsnorm(x, g):
    return x * lax.rsqrt(jnp.mean(x * x, axis=-1, keepdims=True) + RMS_EPS) * g


def _split_hi_lo(x):
    hi = x.astype(BF16)
    lo = (x - hi.astype(F32)).astype(BF16)
    return hi, lo


def _split3(x):
    a = x.astype(BF16).astype(F32)
    r = x - a
    b = r.astype(BF16).astype(F32)
    c = (r - b).astype(BF16).astype(F32)
    return a, b, c


def _proj_kernel(x_ref, g_ref, w_ref, o_ref, xn_ref, *, per_head):
    j = pl.program_id(1)

    @pl.when(j == 0)
    def _():
        xn_ref[...] = _rmsnorm(x_ref[...], g_ref[...]).astype(BF16)

    res = jnp.dot(xn_ref[...], w_ref[...], preferred_element_type=F32)
    if per_head:
        res = res * jnp.where((j == 0) | (j == 3), HEAD_DIM ** -0.5, 1.0)
        for h in range(N_HEADS):
            o_ref[h] = res[:, h * LANES:(h + 1) * LANES].astype(BF16)
    else:
        o_ref[...] = jax.nn.sigmoid(res).astype(BF16)


def _proj(x2d, g, w, *, per_head, tm):
    T, D = x2d.shape
    ncol = w.shape[1] // PROJ_COLS
    if per_head:
        out_shape = jax.ShapeDtypeStruct((ncol * N_HEADS, T, LANES), BF16)
        out_spec = pl.BlockSpec((N_HEADS, tm, LANES), lambda i, j: (j, i, 0))
    else:
        out_shape = jax.ShapeDtypeStruct((T, w.shape[1]), BF16)
        out_spec = pl.BlockSpec((tm, PROJ_COLS), lambda i, j: (i, j))
    return pl.pallas_call(
        functools.partial(_proj_kernel, per_head=per_head),
        grid=(T // tm, ncol),
        in_specs=[pl.BlockSpec((tm, D), lambda i, j: (i, 0)),
                  pl.BlockSpec((1, D), lambda i, j: (0, 0)),
                  pl.BlockSpec((D, PROJ_COLS), lambda i, j: (0, j))],
        out_specs=out_spec,
        out_shape=out_shape,
        scratch_shapes=[pltpu.VMEM((tm, D), BF16)],
        compiler_params=_cparams(("arbitrary", "arbitrary")),
        name="proj_heads" if per_head else "proj_gates",
    )(x2d, g, w)


def _moba_kernel(slope_ref, q_ref, k_ref, v_ref, o_ref, kaug_ref, vaug_ref, kmean_ref, *, nb, groups):
    g = pl.program_id(0)
    i = pl.program_id(1)
    hg, tq, _ = q_ref.shape
    blk = MOBA_BLOCK
    head0 = (g % groups) * hg

    @pl.when(i == 0)
    def _():
        kmean_ref[...] = jnp.zeros_like(kmean_ref)
        row = lax.broadcasted_iota(jnp.int32, (blk, LANES), 0)
        lane = lax.broadcasted_iota(jnp.int32, (blk, LANES), 1)
        ones_q = jnp.where((lane >= LANE_QPOS) & (lane < LANE_QPOS + 3), 1.0, 0.0)
        for hh in range(hg):
            slope = slope_ref[head0 + hh]

            def fill(n, c, hh=hh, slope=slope):
                sl = pl.ds(pl.multiple_of(n * blk, blk), blk)
                kf = k_ref[hh, sl, :].astype(F32)
                kmean_ref[hh, pl.ds(n, 1), :] = jnp.mean(kf, axis=0, keepdims=True)
                pa, pb, pc = _split3(slope * (n * blk + row).astype(F32))
                feat = jnp.where(lane == LANE_BLK0 + n, 1.0, ones_q)
                feat = jnp.where(lane == LANE_KPOS, pa, feat)
                feat = jnp.where(lane == LANE_KPOS + 1, pb, feat)
                feat = jnp.where(lane == LANE_KPOS + 2, pc, feat)
                kaug_ref[hh, sl, :] = jnp.where(lane < HEAD_DIM, kf, feat).astype(BF16)
                vf = v_ref[hh, sl, :].astype(F32)
                vaug_ref[hh, sl, :] = jnp.where(lane == HEAD_DIM, 1.0, vf).astype(BF16)
                return c

            lax.fori_loop(0, nb, fill, 0)

    heads = range(hg)
    lane = lax.broadcasted_iota(jnp.int32, (tq, LANES), 1)
    is_blk_lane = (lane >= LANE_BLK0) & (lane < LANE_BLK0 + MAX_KV_BLOCKS)
    rowb = lax.broadcasted_iota(jnp.int32, (MAX_KV_BLOCKS, tq), 0)
    tpos = (i * blk + lax.broadcasted_iota(jnp.int32, (tq, LANES), 0)).astype(F32)
    own_sl = pl.ds(pl.multiple_of(i * blk, blk), blk)
    r = lax.broadcasted_iota(jnp.int32, (tq, blk), 0)
    c = lax.broadcasted_iota(jnp.int32, (tq, blk), 1)
    causal = c <= r

    qfs = [q_ref[hh].astype(F32) for hh in heads]
    gate = [lax.dot_general(kmean_ref[hh], qfs[hh], NT_DIMS, precision=lax.Precision.HIGHEST,
                            preferred_element_type=F32) for hh in heads]
    qowns, own_scores = [], []
    for hh in heads:
        pa, pb, pc = _split3(-slope_ref[head0 + hh] * tpos)
        feat = jnp.where(lane == LANE_QPOS, pa, jnp.where(lane == LANE_QPOS + 1, pb,
                         jnp.where(lane == LANE_QPOS + 2, pc,
                                   jnp.where((lane >= LANE_KPOS) & (lane < LANE_KPOS + 3), 1.0, 0.0))))
        qown = jnp.where(lane < HEAD_DIM, qfs[hh], feat).astype(BF16)
        qowns.append(qown)
        own_scores.append(lax.dot_general(qown, kaug_ref[hh, own_sl, :], NT_DIMS,
                                          preferred_element_type=F32))
    qaugs = []
    for hh in heads:
        bsm = jnp.where(rowb < i, gate[hh], -jnp.inf)
        sel = jnp.zeros(bsm.shape, jnp.bool_)
        for _ in range(MOBA_TOPK):
            m = jnp.max(bsm, axis=0, keepdims=True)
            p = jnp.min(jnp.where(bsm == m, rowb, MAX_KV_BLOCKS), axis=0, keepdims=True)
            hit = rowb == p
            sel = sel | (hit & (m > -jnp.inf))
            bsm = jnp.where(hit, -jnp.inf, bsm)
        bias_t = jnp.where(sel, 0.0, NEG_BIG)
        bias = jnp.concatenate([jnp.zeros((LANE_BLK0, tq), F32), bias_t,
                                jnp.zeros((LANES - LANE_QPOS, tq), F32)], axis=0).T
        qaugs.append(jnp.where(is_blk_lane, bias.astype(BF16), qowns[hh]))
    states = []
    for hh in heads:
        s0 = jnp.where(causal, own_scores[hh], NEG_BIG)
        m0 = jnp.max(s0, axis=1, keepdims=True)
        p0 = jnp.exp(s0 - m0)
        acc0 = jnp.dot(p0.astype(BF16), vaug_ref[hh, own_sl, :], preferred_element_type=F32)
        states.append((m0, acc0))

    def chunk(first_block, n_blocks, states):
        span = n_blocks * blk
        sl = pl.ds(pl.multiple_of(first_block * blk, blk), span)
        scores = [lax.dot_general(qaugs[hh], kaug_ref[hh, sl, :], NT_DIMS, preferred_element_type=F32)
                  for hh in range(hg)]
        new = []
        for hh in range(hg):
            m, acc = states[hh]
            s = scores[hh]
            mn = jnp.maximum(m, jnp.max(s, axis=1, keepdims=True))
            alpha = jnp.exp(m - mn)
            p = jnp.exp(s - mn)
            acc = alpha * acc + jnp.dot(p.astype(BF16), vaug_ref[hh, sl, :], preferred_element_type=F32)
            new.append((mn, acc))
        return tuple(new)

    full = i // MOBA_KV_CHUNK
    rem = i - full * MOBA_KV_CHUNK
    half = MOBA_KV_CHUNK // 2
    n_chunks = full + (rem > half).astype(jnp.int32)
    states = lax.fori_loop(0, n_chunks, lambda c, st: chunk(c * MOBA_KV_CHUNK, MOBA_KV_CHUNK, st),
                           tuple(states))
    states = lax.cond((rem > 0) & (rem <= half),
                      lambda st: chunk(full * MOBA_KV_CHUNK, half, st), lambda st: st, states)
    for hh in range(hg):
        acc = states[hh][1]
        denom = acc[:, HEAD_DIM:HEAD_DIM + 1]
        o_ref[hh] = jnp.where(lane < HEAD_DIM, acc / denom, 0.0).astype(BF16)


def _moba(qkv, slopes, B, S):
    T = B * S
    nq = S // MOBA_BLOCK
    nb = S // MOBA_BLOCK
    assert nb <= MAX_KV_BLOCKS and nb % MOBA_KV_CHUNK == 0
    H = N_HEADS
    hg = MOBA_HEADS_PER_STEP
    G = H // hg
    return pl.pallas_call(
        functools.partial(_moba_kernel, nb=nb, groups=G),
        grid=(B * G, nq),
        in_specs=[pl.BlockSpec(memory_space=pltpu.SMEM),
                  pl.BlockSpec((hg, MOBA_BLOCK, LANES), lambda g, i: (g % G, (g // G) * nq + i, 0)),
                  pl.BlockSpec((hg, S, LANES), lambda g, i: (G + g % G, g // G, 0),
                               pipeline_mode=pl.Buffered(1)),
                  pl.BlockSpec((hg, S, LANES), lambda g, i: (2 * G + g % G, g // G, 0),
                               pipeline_mode=pl.Buffered(1))],
        out_specs=pl.BlockSpec((hg, MOBA_BLOCK, LANES), lambda g, i: (g % G, (g // G) * nq + i, 0)),
        out_shape=jax.ShapeDtypeStruct((H, T, LANES), BF16),
        scratch_shapes=[pltpu.VMEM((hg, S, LANES), BF16), pltpu.VMEM((hg, S, LANES), BF16),
                        pltpu.VMEM((hg, MAX_KV_BLOCKS, LANES), F32)],
        compiler_params=_cparams(("arbitrary", "arbitrary")),
        name="moba_attn",
    )(slopes, qkv, qkv, qkv)


def _sb_kernel(q_ref, k_ref, v_ref, tri_ref, o_ref):
    i = pl.program_id(1)
    blk = MOBA_BLOCK
    hg, tq, _ = q_ref.shape
    tri = tri_ref[...]

    def tiles(j, states, diag):
        heads = range(hg)
        sl = pl.ds(pl.multiple_of(j * blk, blk), blk)
        zs = [lax.dot_general(q_ref[hh], k_ref[hh, sl, :], NT_DIMS, preferred_element_type=F32)
              for hh in heads]
        if diag:
            r = lax.broadcasted_iota(jnp.int32, (tq, blk), 0)
            c = lax.broadcasted_iota(jnp.int32, (tq, blk), 1)
            strict = c < r
        cums = []
        for hh in heads:
            z = zs[hh]
            lg = jnp.minimum(-z, 0.0) - jnp.log(1.0 + jnp.exp(-jnp.abs(z)))
            if diag:
                lg = jnp.where(strict, lg, 0.0)
            hi, lo = _split_hi_lo(lg)
            cums.append(jnp.dot(hi, tri, preferred_element_type=F32)
                        + jnp.dot(lo, tri, preferred_element_type=F32))
        new = []
        for hh in heads:
            carry, decay, acc = states[hh]
            w = jnp.exp(zs[hh] + cums[hh])
            if diag:
                w = jnp.where(strict, w, 0.0)
            pv = jnp.dot(w.astype(BF16), v_ref[hh, sl, :], preferred_element_type=F32)
            carry = carry + cums[hh][:, 0:1]
            new.append((carry, jnp.exp(carry), acc + decay * pv))
        return tuple(new)

    def any_alive(states):
        alive = jnp.max(states[0][1])
        for st in states[1:]:
            alive = jnp.maximum(alive, jnp.max(st[1]))
        return alive > 0.0

    init = (jnp.zeros((tq, 1), F32), jnp.ones((tq, 1), F32), jnp.zeros((tq, LANES), F32))
    states = tiles(i, (init,) * hg, True)

    def cond(c):
        t, alive, _ = c
        return (t < i) & alive

    def body(c):
        t, _, states = c
        states = tiles(i - 1 - t, states, False)
        return t + 1, any_alive(states), states

    _, _, states = lax.while_loop(cond, body, (jnp.int32(0), any_alive(states), states))
    for hh in range(hg):
        o_ref[hh] = states[hh][2].astype(BF16)


def _sb(qkv, tri, B, S):
    T = B * S
    nq = S // MOBA_BLOCK
    H = N_HEADS
    hg = SB_HEADS_PER_STEP
    G = H // hg
    return pl.pallas_call(
        _sb_kernel,
        grid=(B * G, nq),
        in_specs=[pl.BlockSpec((hg, MOBA_BLOCK, LANES), lambda g, i: (3 * G + g % G, (g // G) * nq + i, 0)),
                  pl.BlockSpec((hg, S, LANES), lambda g, i: (4 * G + g % G, g // G, 0),
                               pipeline_mode=pl.Buffered(1)),
                  pl.BlockSpec((hg, S, LANES), lambda g, i: (5 * G + g % G, g // G, 0),
                               pipeline_mode=pl.Buffered(1)),
                  pl.BlockSpec((MOBA_BLOCK, MOBA_BLOCK), lambda g, i: (0, 0))],
        out_specs=pl.BlockSpec((hg, MOBA_BLOCK, LANES), lambda g, i: (g % G, (g // G) * nq + i, 0)),
        out_shape=jax.ShapeDtypeStruct((H, T, LANES), BF16),
        compiler_params=_cparams(("arbitrary", "arbitrary")),
        name="sb_attn",
    )(qkv, qkv, qkv, tri)


def _mix_kernel(x_ref, oa_ref, ob_ref, gate_ref, woa_ref, wob_ref, wmix_ref, g2_ref, wq_ref,
                h_ref, x2_ref, qp_ref):
    D = x_ref.shape[1]

    def out_proj(o_ref, w_ref):
        y = None
        for p in range(N_HEADS // 2):
            pair = jnp.concatenate([o_ref[2 * p], o_ref[2 * p + 1]], axis=1)
            part = jnp.dot(pair, w_ref[p], preferred_element_type=F32)
            y = part if y is None else y + part
        return y

    ya = out_proj(oa_ref, woa_ref)
    yb = out_proj(ob_ref, wob_ref)
    mixed = gate_ref[:, :D].astype(F32) * ya + gate_ref[:, D:].astype(F32) * yb
    h = x_ref[...] + jnp.dot(mixed.astype(BF16), wmix_ref[...], preferred_element_type=F32)
    h_ref[...] = h
    x2 = _rmsnorm(h, g2_ref[...]).astype(BF16)
    x2_ref[...] = x2
    qp_ref[...] = jnp.dot(x2, wq_ref[...], preferred_element_type=F32)


def _mix(x2d, oa, ob, gates, woa, wob, wmix, g2, wq, *, tm):
    T, D = x2d.shape
    nqp = wq.shape[1]
    H = N_HEADS
    const = lambda *shape: pl.BlockSpec(shape, lambda i: (0,) * len(shape))
    return pl.pallas_call(
        _mix_kernel,
        grid=(T // tm,),
        in_specs=[pl.BlockSpec((tm, D), lambda i: (i, 0)),
                  pl.BlockSpec((H, tm, LANES), lambda i: (0, i, 0)),
                  pl.BlockSpec((H, tm, LANES), lambda i: (0, i, 0)),
                  pl.BlockSpec((tm, 2 * D), lambda i: (i, 0)),
                  const(H // 2, 2 * LANES, D), const(H // 2, 2 * LANES, D), const(D, D), const(1, D),
                  const(D, nqp)],
        out_specs=[pl.BlockSpec((tm, D), lambda i: (i, 0)),
                   pl.BlockSpec((tm, D), lambda i: (i, 0)),
                   pl.BlockSpec((tm, nqp), lambda i: (i, 0))],
        out_shape=[jax.ShapeDtypeStruct((T, D), F32),
                   jax.ShapeDtypeStruct((T, D), BF16),
                   jax.ShapeDtypeStruct((T, nqp), F32)],
        compiler_params=_cparams(("arbitrary",)),
        name="mix_out",
    )(x2d, oa, ob, gates, woa, wob, wmix, g2, wq)


_CAND_SLABS = ((0, 16, 0), (1, 8, 16), (2, 5, 24), (3, 4, 32), (4, 3, 40), (5, 2, 48), (6, 2, 56), (7, 2, 64))
_CAND_TAIL_ROW0 = 72
_CAND_ROWS = 80


def _topk_rank(x, k, top_ref, *, tie_break, track_rank=True):
    R, n = x.shape
    row = lax.broadcasted_iota(jnp.int32, (R, n), 0)
    rank = jnp.full((R, n), k, jnp.int32) if track_rank else None
    vals = []
    for r in range(k):
        m = jnp.max(x, axis=0, keepdims=True)
        hit = x == m
        if tie_break:
            hit = row == jnp.min(jnp.where(hit, row, R), axis=0, keepdims=True)
        if track_rank:
            rank = jnp.where(hit, r, rank)
        x = jnp.where(hit, -jnp.inf, x)
        vals.append(m)
        if top_ref is not None:
            top_ref[r:r + 1, :] = m
    taken = (rank < k) if track_rank else (x == -jnp.inf)
    count = jnp.sum(taken.astype(F32), axis=0, keepdims=True)
    return rank, vals, count


def _route_select(s1, s2, write, top1_ref, top2_ref, cand_ref, *, tie_break):
    K = PEER_TOPK
    n = s1.shape[1]
    rank1, _, n1 = _topk_rank(s1, K, top1_ref, tie_break=tie_break, track_rank=tie_break)
    rank2, _, n2 = _topk_rank(s2, K, top2_ref, tie_break=tie_break)
    t1 = top1_ref[...]
    t2 = top2_ref[...]
    row8 = lax.broadcasted_iota(jnp.int32, (8, n), 0)
    for a, nvalid, off in _CAND_SLABS:
        rows = 16 if nvalid == 16 else 8
        slab = t1[a:a + 1, :] + t2[0:rows, :]
        if nvalid < rows:
            slab = jnp.where(row8 < nvalid, slab, -jnp.inf)
        cand_ref[off:off + rows, :] = slab
    cand_ref[_CAND_TAIL_ROW0:_CAND_ROWS, :] = t1[8:16, :] + t2[0:1, :]

    crank, cvals, nc = _topk_rank(cand_ref[...], K, None, tie_break=tie_break)
    sel = (crank < K).astype(F32)
    z = jnp.zeros((1, n), F32)
    for r in range(K):
        z = z + jnp.exp(cvals[r] - cvals[0])
    lr = jnp.zeros(s1.shape, F32)
    for a in range(K):
        if a < 8:
            _, nvalid, off = _CAND_SLABS[a]
            rows = 16 if nvalid == 16 else 8
            cnt = jnp.sum(sel[off:off + rows, :], axis=0, keepdims=True)
        else:
            cnt = sel[_CAND_TAIL_ROW0 + a - 8:_CAND_TAIL_ROW0 + a - 7, :]
        lr = jnp.where((rank1 == a) if tie_break else (s1 == t1[a:a + 1, :]), cnt, lr)
    write(lr, jnp.exp(s1 - t1[0:1, :]) / z, rank2.astype(F32), jnp.exp(s2 - t2[0:1, :]))
    excess = jnp.maximum(jnp.maximum(jnp.abs(n1 - K), jnp.abs(n2 - K)), jnp.abs(nc - K))
    return jnp.max(excess) == 0.0


BF16_SUBLANES = 16
PEER_ROWBLOCKS_PER_DOT = 2


def _bcast_rows_bf16(row, n):
    tile = jnp.broadcast_to(row, (BF16_SUBLANES, row.shape[1])).astype(BF16)
    return jnp.concatenate([tile] * (n // BF16_SUBLANES), axis=0)


def _peer_kernel(qp_ref, sk_ref, x2_ref, h_ref, u_ref, v_ref, gf_ref, o_ref,
                 acc_ref, cnt_s, e1_s, r2p_s, e2p_s, top1_ref, top2_ref, cand_ref,
                 *, n_i, n_tiles, final_norm):
    i = pl.program_id(0)
    e = pl.program_id(1)
    last = pl.num_programs(1) - 1
    wslot = i % 2
    rslot = 1 - wslot
    NK = PEER_NKEYS
    tt = x2_ref.shape[0]

    def scores():
        q = qp_ref[...]
        half = q.shape[1] // 2
        s1 = lax.dot_general(sk_ref[0, 0], q[:, :half], NT_DIMS, precision=lax.Precision.HIGHEST,
                             preferred_element_type=F32)
        s2 = lax.dot_general(sk_ref[0, 1], q[:, half:], NT_DIMS, precision=lax.Precision.HIGHEST,
                             preferred_element_type=F32)
        return s1, s2

    def write(cnt, e1, rank2, e2):
        cnt_s[wslot, e] = cnt
        e1_s[wslot, e] = e1
        r2p_s[wslot, e] = pltpu.bitcast(rank2.astype(BF16), jnp.uint32)
        e2p_s[wslot, e] = pltpu.bitcast(e2.astype(BF16), jnp.uint32)

    def route(s1, s2, tie_break):
        return _route_select(s1, s2, write, top1_ref, top2_ref, cand_ref, tie_break=tie_break)

    def route_exact_if_tied(s1, s2, tie_free):
        @pl.when(jnp.logical_not(tie_free))
        def _():
            route(s1, s2, True)

    def experts():
        zero = jnp.zeros((NK, LANES), BF16)
        cnts = [[cnt_s[rslot, hh, pl.ds(e * n_i + ii, 1), :] for hh in range(PEER_HEADS)]
                for ii in range(n_i)]
        e1s = [[e1_s[rslot, hh, pl.ds(e * n_i + ii, 1), :] for hh in range(PEER_HEADS)]
               for ii in range(n_i)]
        x2 = x2_ref[...]
        w_rows = []
        for ii in range(n_i):
            if ii % PEER_ROWBLOCKS_PER_DOT == 0:
                r0 = ii * NK
                at = lax.dot_general(u_ref[r0:r0 + PEER_ROWBLOCKS_PER_DOT * NK, :], x2, NT_DIMS,
                                     preferred_element_type=F32)
            ar0 = (ii % PEER_ROWBLOCKS_PER_DOT) * NK
            w_cols = []
            for c0 in range(0, tt, LANES):
                cols = slice(c0, c0 + LANES)
                a = at[ar0:ar0 + NK, cols]
                ga = (a + a * lax.erf(a)).astype(BF16)
                coef = None
                for hh in range(PEER_HEADS):
                    cnt = _bcast_rows_bf16(cnts[ii][hh][:, cols], NK)
                    e1 = _bcast_rows_bf16(e1s[ii][hh][:, cols], NK)
                    r2 = pltpu.bitcast(r2p_s[rslot, hh, :, cols], BF16)
                    e2 = pltpu.bitcast(e2p_s[rslot, hh, :, cols], BF16)
                    term = jnp.where(r2 < cnt, e2, zero) * e1
                    coef = term if coef is None else coef + term
                w_cols.append(coef * ga)
            w_rows.append(jnp.concatenate(w_cols, axis=1))
        w = jnp.concatenate(w_rows, axis=0)
        acc_ref[...] += lax.dot_general(v_ref[...], w, (((0,), (0,)), ((), ())),
                                        preferred_element_type=F32)

    def finish_tile():
        @pl.when(e == last)
        def _():
            y = h_ref[...] + acc_ref[...].T
            o_ref[...] = _rmsnorm(y, gf_ref[...]) if final_norm else y
            acc_ref[...] = jnp.zeros_like(acc_ref)

    @pl.when(i == 0)
    def _():
        @pl.when(e == 0)
        def _():
            acc_ref[...] = jnp.zeros_like(acc_ref)

        s1, s2 = scores()
        route_exact_if_tied(s1, s2, route(s1, s2, False))

    @pl.when((i > 0) & (i < n_tiles))
    def _():
        s1, s2 = scores()
        tie_free = route(s1, s2, False)
        experts()
        route_exact_if_tied(s1, s2, tie_free)
        finish_tile()

    @pl.when(i == n_tiles)
    def _():
        experts()
        finish_tile()


def _peer(qp, sub_keys, x2, h, u, v, gf, *, tt, eb, final_norm):
    T, D = h.shape
    NE = u.shape[0]
    PH, NK = PEER_HEADS, PEER_NKEYS
    n_tiles = T // tt
    assert NE // eb == PH
    qd = qp.shape[1] // PH
    prev = lambda i: jnp.maximum(i - 1, 0)
    blk = lambda i, e: jnp.where(i == 0, 0, e)
    return pl.pallas_call(
        functools.partial(_peer_kernel, n_i=eb // NK, n_tiles=n_tiles, final_norm=final_norm),
        grid=(n_tiles + 1, PH),
        in_specs=[pl.BlockSpec((tt, qd), lambda i, e: (jnp.minimum(i, n_tiles - 1), e)),
                  pl.BlockSpec((1, 2, NK, qd // 2), lambda i, e: (e, 0, 0, 0)),
                  pl.BlockSpec((tt, D), lambda i, e: (prev(i), 0)),
                  pl.BlockSpec((tt, D), lambda i, e: (prev(i), 0)),
                  pl.BlockSpec((eb, D), lambda i, e: (blk(i, e), 0)),
                  pl.BlockSpec((eb, D), lambda i, e: (blk(i, e), 0)),
                  pl.BlockSpec((1, D), lambda i, e: (0, 0))],
        out_specs=pl.BlockSpec((tt, D), lambda i, e: (prev(i), 0)),
        out_shape=jax.ShapeDtypeStruct((T, D), F32),
        scratch_shapes=[pltpu.VMEM((D, tt), F32),
                        pltpu.VMEM((2, PH, NK, tt), F32), pltpu.VMEM((2, PH, NK, tt), F32),
                        pltpu.VMEM((2, PH, NK // 2, tt), jnp.uint32),
                        pltpu.VMEM((2, PH, NK // 2, tt), jnp.uint32),
                        pltpu.VMEM((PEER_TOPK, tt), F32), pltpu.VMEM((PEER_TOPK, tt), F32),
                        pltpu.VMEM((_CAND_ROWS, tt), F32)],
        compiler_params=_cparams(("arbitrary", "arbitrary")),
        name="peer_route_experts",
    )(qp, sub_keys, x2, h, u, v, gf)


def _pad_heads_cols(w):
    D, n = w.shape
    w = w.reshape(D, n // HEAD_DIM, HEAD_DIM)
    w = jnp.pad(w, ((0, 0), (0, 0), (0, LANES - HEAD_DIM)))
    return w.reshape(D, (n // HEAD_DIM) * LANES)


def _pad_heads_rows(w):
    n, D = w.shape
    w = w.reshape(n // HEAD_DIM, HEAD_DIM, D)
    w = jnp.pad(w, ((0, 0), (0, LANES - HEAD_DIM), (0, 0)))
    return w.reshape(n // HEAD_DIM // 2, 2 * LANES, D)


def kernel(x, norm1_g, w_in, w_out_moba, w_out_sb, w_mix_out, norm2_g, peer_w_q, peer_sub_keys, peer_u,
           peer_v, final_norm_g):
    B, S, D = x.shape
    T = B * S
    depth = w_in.shape[0]
    n_qkv = 6 * N_HEADS * HEAD_DIM
    tm = min(1024, T)
    slopes = jnp.exp2(-8.0 * jnp.arange(1, N_HEADS + 1, dtype=F32) / N_HEADS)
    tri = jnp.tril(jnp.ones((MOBA_BLOCK, MOBA_BLOCK), F32)).astype(BF16)

    h = x.reshape(T, D)
    out = h
    for l in range(depth):
        w_heads = _pad_heads_cols(w_in[l][:, :n_qkv]).astype(BF16)
        w_gates = w_in[l][:, n_qkv:].astype(BF16)
        g1 = norm1_g[l].reshape(1, D)
        qkv = _proj(h, g1, w_heads, per_head=True, tm=tm)
        gates = _proj(h, g1, w_gates, per_head=False, tm=tm)
        oa = _moba(qkv, slopes, B, S)
        ob = _sb(qkv, tri, B, S)
        h, x2, qp = _mix(h, oa, ob, gates,
                         _pad_heads_rows(w_out_moba[l]).astype(BF16),
                         _pad_heads_rows(w_out_sb[l]).astype(BF16),
                         w_mix_out[l].astype(BF16), norm2_g[l].reshape(1, D),
                         peer_w_q[l].astype(BF16), tm=min(512, T))
        out = _peer(qp, peer_sub_keys[l], x2, h, (peer_u[l] * INV_SQRT2).astype(BF16),
                    (peer_v[l] * INV_SQRT2).astype(BF16),
                    final_norm_g.reshape(1, D), tt=min(512, T), eb=2048, final_norm=(l == depth - 1))
        h = out
    return out.reshape(B, S, D)
```
